```python
import math
import jax, jax.numpy as jnp
from jax import lax
import numpy as np

D_MODEL = 1024
BATCH = 8
SEQ = 4096
DEPTH = 2

PLE_DIM = 256
N_BRANCHES = 4
BRANCH_WIDTH = D_MODEL // 4
HEAD_DIM = 64
N_HEADS = BRANCH_WIDTH // HEAD_DIM
S5_GROUP_CH = 16
S5_GROUPS = BRANCH_WIDTH // S5_GROUP_CH
S5_STATE = 64
RWKV_DECAY_LORA = 64
RWKV_AAA_LORA = 64
RWKV_GATE_LORA = 128
RWKV_PROJ = 3 * BRANCH_WIDTH + RWKV_DECAY_LORA + RWKV_AAA_LORA + RWKV_GATE_LORA
D_FF = 4 * D_MODEL
QBLOCK = 128
LN_EPS = 1e-5
GN_EPS = 64e-5
DEEPNORM_ALPHA = (2 * DEPTH) ** 0.25
DEEPNORM_BETA = (8 * DEPTH) ** -0.25
IN_SPLIT_SIZES = (BRANCH_WIDTH, BRANCH_WIDTH, BRANCH_WIDTH, BRANCH_WIDTH, N_HEADS, RWKV_PROJ,
                  BRANCH_WIDTH, BRANCH_WIDTH, BRANCH_WIDTH, N_BRANCHES * D_MODEL)
D_IN = sum(IN_SPLIT_SIZES)
RWKV_SPLIT_SIZES = (BRANCH_WIDTH, BRANCH_WIDTH, BRANCH_WIDTH, RWKV_DECAY_LORA, RWKV_AAA_LORA, RWKV_GATE_LORA)

kernel_name = 'hybrid_s5_fox_rwkv7_stickbreaking_deepnorm'


def _split(t, sizes):
    points = np.cumsum(sizes)[:-1].tolist()
    return jnp.split(t, points, axis=-1)


def _layer_norm(t, g, b):
    tf = t.astype(jnp.float32)
    mu = jnp.mean(tf, axis=-1, keepdims=True)
    var = jnp.mean(jnp.square(tf - mu), axis=-1, keepdims=True)
    return ((tf - mu) * lax.rsqrt(var + LN_EPS) * g + b).astype(t.dtype)


def _heads(t):
    return t.reshape(t.shape[:-1] + (N_HEADS, HEAD_DIM))


def _query_blocks(t):
    b, s = t.shape[:2]
    t = t.reshape((b, s // QBLOCK, QBLOCK) + t.shape[2:])
    return jnp.moveaxis(t, 1, 0)


def _merge_blocks(t):
    t = jnp.moveaxis(t, 0, 1)
    return t.reshape((t.shape[0], t.shape[1] * t.shape[2]) + t.shape[3:])


def _s5_mixer(u, lam_re, lam_im, log_dt, b_re, b_im, c_re, c_im, d, glu_w, glu_b):
    bsz, s = u.shape[:2]
    f32 = jnp.float32
    ug = u.astype(f32).reshape(bsz, s, S5_GROUPS, S5_GROUP_CH)
    dt = jnp.exp(log_dt.astype(f32))[:, None]
    lam_re = lam_re.astype(f32)
    lam_im = lam_im.astype(f32)
    mag = jnp.exp(lam_re * dt)
    ang = lam_im * dt
    lb_re = mag * jnp.cos(ang)
    lb_im = mag * jnp.sin(ang)
    den = jnp.square(lam_re) + jnp.square(lam_im)
    nr = lb_re - 1.0
    f_re = (nr * lam_re + lb_im * lam_im) / den
    f_im = (lb_im * lam_re - nr * lam_im) / den
    bb_re = f_re[..., None] * b_re - f_im[..., None] * b_im
    bb_im = f_re[..., None] * b_im + f_im[..., None] * b_re
    bu_re = jnp.einsum('bsgh,gph->bsgp', ug, bb_re)
    bu_im = jnp.einsum('bsgh,gph->bsgp', ug, bb_im)
    a_re = jnp.broadcast_to(lb_re, bu_re.shape)
    a_im = jnp.broadcast_to(lb_im, bu_im.shape)

    def combine(e1, e2):
        a1r, a1i, b1r, b1i = e1
        a2r, a2i, b2r, b2i = e2
        return (a2r * a1r - a2i * a1i,
                a2r * a1i + a2i * a1r,
                a2r * b1r - a2i * b1i + b2r,
                a2r * b1i + a2i * b1r + b2i)

    _, _, x_re, x_im = lax.associative_scan(combine, (a_re, a_im, bu_re, bu_im), axis=1)
    y = (jnp.einsum('bsgp,ghp->bsgh', x_re, c_re) - jnp.einsum('bsgp,ghp->bsgh', x_im, c_im)
         + d * ug)
    y = jax.nn.gelu(y.reshape(bsz, s, BRANCH_WIDTH))
    y = y * jax.nn.sigmoid(y @ glu_w + glu_b)
    return y.astype(u.dtype)


def _forgetting_attention(q, k, v, f_logit):
    s = q.shape[1]
    f32 = jnp.float32
    scale = HEAD_DIM ** -0.5
    c = jnp.cumsum(jax.nn.log_sigmoid(f_logit.astype(f32)), axis=1)
    kf = k.astype(f32)
    vf = v.astype(f32)
    c_key = jnp.moveaxis(c, 1, 2)
    kpos = jnp.arange(s)

    def block(args):
        i, qb, cb = args
        qpos = i * QBLOCK + jnp.arange(QBLOCK)
        logits = jnp.einsum('bqhd,bkhd->bhqk', qb.astype(f32), kf) * scale
        logits = logits + jnp.moveaxis(cb, 1, 2)[..., None] - c_key[:, :, None, :]
        logits = jnp.where(kpos[None, :] <= qpos[:, None], logits, -jnp.inf)
        probs = jax.nn.softmax(logits, axis=-1)
        return jnp.einsum('bhqk,bkhd->bqhd', probs, vf)

    out = lax.map(block, (jnp.arange(s // QBLOCK), _query_blocks(q), _query_blocks(c)))
    return _merge_blocks(out).astype(q.dtype)


def _stick_breaking_attention(q, k, v):
    s = q.shape[1]
    f32 = jnp.float32
    scale = HEAD_DIM ** -0.5
    kf = k.astype(f32)
    vf = v.astype(f32)
    kpos = jnp.arange(s)

    def block(args):
        i, qb = args
        qpos = i * QBLOCK + jnp.arange(QBLOCK)
        z = jnp.einsum('bqhd,bkhd->bhqk', qb.astype(f32), kf) * scale
        mask = kpos[None, :] < qpos[:, None]
        log_rest = jnp.where(mask, jax.nn.log_sigmoid(-z), 0.0)
        between = lax.cumsum(log_rest, axis=3, reverse=True) - log_rest
        weights = jnp.where(mask, jnp.exp(jax.nn.log_sigmoid(z) + between), 0.0)
        return jnp.einsum('bhqk,bkhd->bqhd', weights, vf)

    out = lax.map(block, (jnp.arange(s // QBLOCK), _query_blocks(q)))
    return _merge_blocks(out).astype(q.dtype)


def _rwkv7_time_mix(proj, mu, w0, w2, a0, a2, g2, k_k, k_a, r_k, lnx_g, lnx_b):
    bsz = proj.shape[0]
    f32 = jnp.float32
    prev = jnp.pad(proj, ((0, 0), (1, 0), (0, 0)))[:, :-1]
    xs = proj + (prev - proj) * mu
    r, k, v, w1, a1, g1 = _split(xs, RWKV_SPLIT_SIZES)
    w = -jax.nn.softplus(-(w0 + jnp.tanh(w1) @ w2)) - 0.5
    decay = jnp.exp(-jnp.exp(w.astype(f32)))
    a = jax.nn.sigmoid(a0 + a1 @ a2)
    g = jax.nn.sigmoid(g1) @ g2
    kk = _heads((k * k_k).astype(f32))
    kk = kk / jnp.maximum(jnp.sqrt(jnp.sum(jnp.square(kk), axis=-1, keepdims=True)), 1e-12)
    k = k * (1.0 + (a - 1.0) * k_a)
    rh, kh, vh, ah, dh = [_heads(t.astype(f32)) for t in (r, k, v, a, decay)]

    def step(state, inp):
        r_t, w_t, k_t, v_t, kk_t, a_t = inp
        sa = jnp.einsum('bhvk,bhk->bhv', state, -kk_t)
        state = (state * w_t[:, :, None, :] + sa[..., None] * (kk_t * a_t)[:, :, None, :]
                 + v_t[..., None] * k_t[:, :, None, :])
        return state, jnp.einsum('bhvk,bhk->bhv', state, r_t)

    tm = lambda t: jnp.moveaxis(t, 1, 0)
    init = jnp.zeros((bsz, N_HEADS, HEAD_DIM, HEAD_DIM), f32)
    _, y = lax.scan(step, init, (tm(rh), tm(dh), tm(kh), tm(vh), tm(kk), tm(ah)))
    y = jnp.moveaxis(y, 0, 1)
    ym = jnp.mean(y, axis=-1, keepdims=True)
    yv = jnp.mean(jnp.square(y - ym), axis=-1, keepdims=True)
    yn = (y - ym) * lax.rsqrt(yv + GN_EPS)
    yn = yn.reshape(y.shape[:2] + (BRANCH_WIDTH,)) * lnx_g + lnx_b
    bonus = (jnp.sum(rh * kh * r_k, axis=-1, keepdims=True) * vh).reshape(yn.shape)
    return ((yn + bonus) * g).astype(proj.dtype)


def setup_inputs(seed: int = 0) -> dict:
    key = jax.random.key(seed)
    keys = iter(jax.random.split(key, 40))
    f32 = jnp.float32
    nrm = lambda shape, scale: scale * jax.random.normal(next(keys), shape, f32)
    uni = lambda shape, lo, hi: jax.random.uniform(next(keys), shape, f32, minval=lo, maxval=hi)
    L, G, P, H16, BW = DEPTH, S5_GROUPS, S5_STATE, S5_GROUP_CH, BRANCH_WIDTH
    beta = DEEPNORM_BETA
    return {
        'x': nrm((BATCH, SEQ, D_MODEL), 1.0),
        'p': nrm((DEPTH, BATCH, SEQ, PLE_DIM), 1.0),
        'w_in': nrm((L, D_MODEL, D_IN), D_MODEL ** -0.5),
        's5_lambda_re': -0.5 + nrm((L, G, P), 0.01),
        's5_lambda_im': jnp.broadcast_to(math.pi * jnp.arange(P, dtype=f32), (L, G, P)),
        's5_log_dt': uni((L, G), math.log(1e-3), math.log(1e-1)),
        's5_b_re': nrm((L, G, P, H16), (2.0 * H16) ** -0.5),
        's5_b_im': nrm((L, G, P, H16), (2.0 * H16) ** -0.5),
        's5_c_re': nrm((L, G, H16, P), (2.0 * P) ** -0.5),
        's5_c_im': nrm((L, G, H16, P), (2.0 * P) ** -0.5),
        's5_d': nrm((L, G, H16), 1.0),
        's5_glu_w': nrm((L, BW, BW), BW ** -0.5),
        's5_glu_b': nrm((L, BW), 0.01),
        'fox_f_bias': 3.0 + nrm((L, N_HEADS), 0.5),
        'rwkv_mu': uni((L, RWKV_PROJ), 0.0, 1.0),
        'rwkv_w0': uni((L, BW), -6.0, 1.0),
        'rwkv_w2': nrm((L, RWKV_DECAY_LORA, BW), 0.3 * RWKV_DECAY_LORA ** -0.5),
        'rwkv_a0': nrm((L, BW), 0.1),
        'rwkv_a2': nrm((L, RWKV_AAA_LORA, BW), 0.3 * RWKV_AAA_LORA ** -0.5),
        'rwkv_g2': nrm((L, RWKV_GATE_LORA, BW), RWKV_GATE_LORA ** -0.5),
        'rwkv_k_k': 0.85 + nrm((L, BW), 0.05),
        'rwkv_k_a': 1.0 + nrm((L, BW), 0.05),
        'rwkv_r_k': nrm((L, N_HEADS, HEAD_DIM), 0.1),
        'rwkv_lnx_g': 1.0 + nrm((L, BW), 0.01),
        'rwkv_lnx_b': nrm((L, BW), 0.01),
        'w_branch': nrm((L, N_BRANCHES, BW, D_MODEL), beta * BW ** -0.5),
        'w_out': nrm((L, D_MODEL, D_MODEL), beta * D_MODEL ** -0.5),
        'ln1_g': 1.0 + nrm((L, D_MODEL), 0.01),
        'ln1_b': nrm((L, D_MODEL), 0.01),
        'mlp_w1': nrm((L, D_MODEL, D_FF), beta * D_MODEL ** -0.5),
        'mlp_w2': nrm((L, D_FF, D_MODEL), beta * D_FF ** -0.5),
        'ple_w': nrm((L, PLE_DIM, D_MODEL), beta * PLE_DIM ** -0.5),
        'ple_gate_w': nrm((L, D_MODEL, D_MODEL), D_MODEL ** -0.5),
        'ln2_g': 1.0 + nrm((L, D_MODEL), 0.01),
        'ln2_b': nrm((L, D_MODEL), 0.01),
    }


def reference(x, p, w_in, s5_lambda_re, s5_lambda_im, s5_log_dt, s5_b_re, s5_b_im, s5_c_re, s5_c_im,
              s5_d, s5_glu_w, s5_glu_b, fox_f_bias, rwkv_mu, rwkv_w0, rwkv_w2, rwkv_a0, rwkv_a2, rwkv_g2,
              rwkv_k_k, rwkv_k_a, rwkv_r_k, rwkv_lnx_g, rwkv_lnx_b, w_branch, w_out, ln1_g, ln1_b,
              mlp_w1, mlp_w2, ple_w, ple_gate_w, ln2_g, ln2_b):
    bsz, s, _ = x.shape
    h = x
    for i in range(DEPTH):
        proj = h @ w_in[i]
        (s5_u, fq, fk, fv, ff, rw, sq, sk, sv, gates) = _split(proj, IN_SPLIT_SIZES)
        y_s5 = _s5_mixer(s5_u, s5_lambda_re[i], s5_lambda_im[i], s5_log_dt[i], s5_b_re[i], s5_b_im[i],
                         s5_c_re[i], s5_c_im[i], s5_d[i], s5_glu_w[i], s5_glu_b[i])
        y_fox = _forgetting_attention(_heads(fq), _heads(fk), _heads(fv),
                                      ff + fox_f_bias[i]).reshape(bsz, s, BRANCH_WIDTH)
        y_rwkv = _rwkv7_time_mix(rw, rwkv_mu[i], rwkv_w0[i], rwkv_w2[i], rwkv_a0[i], rwkv_a2[i],
                                 rwkv_g2[i], rwkv_k_k[i], rwkv_k_a[i], rwkv_r_k[i],
                                 rwkv_lnx_g[i], rwkv_lnx_b[i])
        y_sb = _stick_breaking_attention(_heads(sq), _heads(sk), _heads(sv)).reshape(bsz, s, BRANCH_WIDTH)
        gates = gates.reshape(bsz, s, N_BRANCHES, D_MODEL)
        branches = (y_s5, y_fox, y_rwkv, y_sb)
        merged = sum(jax.nn.sigmoid(gates[:, :, n]) * (branches[n] @ w_branch[i, n])
                     for n in range(N_BRANCHES))
        h = _layer_norm(DEEPNORM_ALPHA * h + merged @ w_out[i], ln1_g[i], ln1_b[i])
        ffn = jnp.square(jax.nn.relu(h @ mlp_w1[i])) @ mlp_w2[i]
        ple = jax.nn.sigmoid(h @ ple_gate_w[i]) * (p[i] @ ple_w[i])
        h = _layer_norm(DEEPNORM_ALPHA * h + ffn + ple, ln2_g[i], ln2_b[i])
    return h
```

```python
import functools
import math

import jax
import jax.numpy as jnp
from jax import lax
from jax.experimental import pallas as pl
from jax.experimental.pallas import tpu as pltpu

F32 = jnp.float32
BF16 = jnp.bfloat16
HI = lax.Precision.HIGHEST

D_MODEL = 1024
PLE_DIM = 256
N_BRANCHES = 4
BW = 256
HEAD_DIM = 64
N_HEADS = 4
S5_GROUP_CH = 16
S5_GROUPS = 16
S5_STATE = 64
S5_N = S5_GROUPS * S5_STATE
RWKV_PROJ = 1024
D_FF = 4096
LN_EPS = 1e-5
GN_EPS = 64e-5
DEPTH = 2
ALPHA = (2 * DEPTH) ** 0.25
LANE = 128
RWKV_CHUNK = 64
VMEM_LIMIT = 56 * 1024 * 1024


def _cp(sem, vmem=VMEM_LIMIT):
    return pltpu.CompilerParams(dimension_semantics=sem, vmem_limit_bytes=vmem)


def _dot(a, b):
    return jnp.dot(a.astype(BF16), b.astype(BF16), preferred_element_type=F32)


def _dot_hi(a, b):
    return jnp.dot(a, b, precision=HI, preferred_element_type=F32)


def _dot_nt(a, b):
    return lax.dot_general(a.astype(BF16), b.astype(BF16), (((1,), (1,)), ((), ())),
                           preferred_element_type=F32)


def _dot_tn(a, b):
    return lax.dot_general(a.astype(BF16), b.astype(BF16), (((0,), (0,)), ((), ())),
                           preferred_element_type=F32)


def _softplus(x):
    return jnp.maximum(x, 0.0) + jnp.log1p(jnp.exp(-jnp.abs(x)))


def _gelu_tanh(x):
    return 0.5 * x * (1.0 + jnp.tanh(math.sqrt(2.0 / math.pi) * (x + 0.044715 * (x * x * x))))


def _layer_norm(z, g, b):
    mu = jnp.mean(z, axis=-1, keepdims=True)
    d = z - mu
    var = jnp.mean(d * d, axis=-1, keepdims=True)
    return d * lax.rsqrt(var + LN_EPS) * g + b


def _proj_kernel(h_ref, wm_ref, wf_ref, u_ref, fqkv_ref, rw_ref, sqkv_ref, fft_ref):
    hb = h_ref[...].astype(BF16)
    u_ref[...] = jnp.dot(hb, wm_ref[:, 0:256], preferred_element_type=F32)
    fqkv_ref[...] = jnp.dot(hb, wm_ref[:, 256:1024], preferred_element_type=F32).astype(BF16)
    rw_ref[...] = jnp.dot(hb, wm_ref[:, 1024:2048], preferred_element_type=F32)
    sqkv_ref[...] = jnp.dot(hb, wm_ref[:, 2048:2816], preferred_element_type=F32).astype(BF16)
    fft_ref[0] = _dot_nt(wf_ref[...], hb)


def _proj(h, wm, wf, bsz, seq, tm):
    ns = seq // tm
    t = bsz * seq
    row = lambda b, i: (b * ns + i, 0)
    return pl.pallas_call(
        _proj_kernel,
        grid=(bsz, ns),
        in_specs=[pl.BlockSpec((tm, D_MODEL), row),
                  pl.BlockSpec((D_MODEL, 2816), lambda b, i: (0, 0)),
                  pl.BlockSpec((8, D_MODEL), lambda b, i: (0, 0))],
        out_specs=[pl.BlockSpec((tm, BW), lambda b, i: (i, b)),
                   pl.BlockSpec((tm, 3 * BW), row),
                   pl.BlockSpec((tm, RWKV_PROJ), row),
                   pl.BlockSpec((tm, 3 * BW), row),
                   pl.BlockSpec((1, 8, tm), lambda b, i: (b, 0, i))],
        out_shape=[jax.ShapeDtypeStruct((seq, bsz * BW), F32),
                   jax.ShapeDtypeStruct((t, 3 * BW), BF16),
                   jax.ShapeDtypeStruct((t, RWKV_PROJ), F32),
                   jax.ShapeDtypeStruct((t, 3 * BW), BF16),
                   jax.ShapeDtypeStruct((bsz, 8, seq), F32)],
        compiler_params=_cp(("parallel", "parallel")),
        name="in_proj",
    )(h, wm, wf)


def _gates_kernel(h_ref, wg_ref, o_ref):
    hb = h_ref[...].astype(BF16)
    for c in range(N_BRANCHES):
        sl = slice(c * D_MODEL, (c + 1) * D_MODEL)
        o_ref[:, sl] = jax.nn.sigmoid(
            jnp.dot(hb, wg_ref[:, sl], preferred_element_type=F32)).astype(BF16)


def _gates(h, wg, tm):
    t = h.shape[0]
    return pl.pallas_call(
        _gates_kernel,
        grid=(t // tm,),
        in_specs=[pl.BlockSpec((tm, D_MODEL), lambda i: (i, 0)),
                  pl.BlockSpec((D_MODEL, N_BRANCHES * D_MODEL), lambda i: (0, 0))],
        out_specs=pl.BlockSpec((tm, N_BRANCHES * D_MODEL), lambda i: (i, 0)),
        out_shape=jax.ShapeDtypeStruct((t, N_BRANCHES * D_MODEL), BF16),
        compiler_params=_cp(("parallel",)),
        name="gate_proj",
    )(h, wg)


def _s5_kernel(u_ref, bbar_ref, cmat_ref, lam_ref, d_ref, gw_ref, gb_ref, y_ref, xs_scr, st_scr,
               *, ts, bsz):
    @pl.when(pl.program_id(0) == 0)
    def _():
        st_scr[...] = jnp.zeros(st_scr.shape, F32)

    u = u_ref[...]
    xs_scr[...] = jnp.dot(u.astype(BF16), bbar_ref[...], preferred_element_type=F32)
    lr = lam_ref[:, 0:S5_N]
    li = lam_ref[:, S5_N:2 * S5_N]

    def body(t, carry):
        xr, xi = carry
        r0 = pl.multiple_of(t * bsz, bsz)
        br = xs_scr[pl.ds(r0, bsz), 0:S5_N]
        bi = xs_scr[pl.ds(r0, bsz), S5_N:2 * S5_N]
        nr = lr * xr - li * xi + br
        ni = lr * xi + li * xr + bi
        xs_scr[pl.ds(r0, bsz), 0:S5_N] = nr
        xs_scr[pl.ds(r0, bsz), S5_N:2 * S5_N] = ni
        return nr, ni

    xr, xi = lax.fori_loop(0, ts, body, (st_scr[:, 0:S5_N], st_scr[:, S5_N:2 * S5_N]), unroll=4)
    st_scr[:, 0:S5_N] = xr
    st_scr[:, S5_N:2 * S5_N] = xi

    y = jnp.dot(xs_scr[...].astype(BF16), cmat_ref[...], preferred_element_type=F32) + d_ref[...] * u
    y = _gelu_tanh(y)
    y_ref[...] = y * jax.nn.sigmoid(_dot(y, gw_ref[...]) + gb_ref[...])


def _s5(u_sb, bbar, cmat, lam, dvec, gw, gb, bsz, seq, ts):
    rows = ts * bsz
    full = lambda shape: pl.BlockSpec(shape, lambda i: (0, 0))
    return pl.pallas_call(
        functools.partial(_s5_kernel, ts=ts, bsz=bsz),
        grid=(seq // ts,),
        in_specs=[pl.BlockSpec((rows, BW), lambda i: (i, 0)),
                  full((BW, 2 * S5_N)), full((2 * S5_N, BW)), full((bsz, 2 * S5_N)),
                  full((1, BW)), full((BW, BW)), full((1, BW))],
        out_specs=pl.BlockSpec((rows, BW), lambda i: (i, 0)),
        out_shape=jax.ShapeDtypeStruct((seq * bsz, BW), F32),
        scratch_shapes=[pltpu.VMEM((rows, 2 * S5_N), F32), pltpu.VMEM((bsz, 2 * S5_N), F32)],
        compiler_params=_cp(("arbitrary",)),
        name="s5_mixer",
    )(u_sb, bbar, cmat, lam, dvec, gw, gb)


def _s5_params(lam_re, lam_im, log_dt, b_re, b_im, c_re, c_im, bsz):
    dt = jnp.exp(log_dt)[:, None]
    mag = jnp.exp(lam_re * dt)
    ang = lam_im * dt
    lb_re = mag * jnp.cos(ang)
    lb_im = mag * jnp.sin(ang)
    den = jnp.square(lam_re) + jnp.square(lam_im)
    nr = lb_re - 1.0
    f_re = (nr * lam_re + lb_im * lam_im) / den
    f_im = (lb_im * lam_re - nr * lam_im) / den
    bb_re = f_re[..., None] * b_re - f_im[..., None] * b_im
    bb_im = f_re[..., None] * b_im + f_im[..., None] * b_re
    eye = jnp.eye(S5_GROUPS, dtype=F32)
    bd_in = lambda m: jnp.einsum('gph,gk->ghkp', m, eye).reshape(BW, S5_N)
    bd_out = lambda m: jnp.einsum('ghp,gk->gpkh', m, eye).reshape(S5_N, BW)
    bbar = jnp.concatenate([bd_in(bb_re), bd_in(bb_im)], axis=1).astype(BF16)
    cmat = jnp.concatenate([bd_out(c_re), -bd_out(c_im)], axis=0).astype(BF16)
    lam = jnp.concatenate([lb_re.reshape(1, S5_N), lb_im.reshape(1, S5_N)], axis=1)
    return bbar, cmat, jnp.broadcast_to(lam, (bsz, 2 * S5_N))


def _foxc_kernel(ff_ref, bias_ref, tri_ref, c_ref, *, n_chunks):
    carry = jnp.zeros((8, 1), F32)
    for ch in range(n_chunks):
        sl = slice(ch * LANE, (ch + 1) * LANE)
        x = ff_ref[0, :, sl] + bias_ref[...]
        lf = jnp.minimum(x, 0.0) - jnp.log1p(jnp.exp(-jnp.abs(x)))
        cs = _dot_hi(lf, tri_ref[...]) + carry
        c_ref[0, :, sl] = cs
        carry = cs[:, LANE - 1:LANE]


def _fox_c(fft, bias8, tri):
    bsz, _, seq = fft.shape
    return pl.pallas_call(
        functools.partial(_foxc_kernel, n_chunks=seq // LANE),
        grid=(bsz,),
        in_specs=[pl.BlockSpec((1, 8, seq), lambda b: (b, 0, 0)),
                  pl.BlockSpec((8, 1), lambda b: (0, 0)),
                  pl.BlockSpec((LANE, LANE), lambda b: (0, 0))],
        out_specs=pl.BlockSpec((1, 8, seq), lambda b: (b, 0, 0)),
        out_shape=jax.ShapeDtypeStruct((bsz, 8, seq), F32),
        compiler_params=_cp(("parallel",)),
        name="fox_cumlogf",
    )(fft, bias8, tri)


def _head_select(x, hh):
    lane = lax.broadcasted_iota(jnp.int32, x.shape, 1)
    sel = (lane < HEAD_DIM) if hh == 0 else (lane >= HEAD_DIM)
    return jnp.where(sel, x, jnp.zeros_like(x))


def _fox_kernel(q_ref, k_ref, v_ref, ccol_ref, crow_ref, o_ref, m_scr, l_scr, acc_scr):
    i = pl.program_id(2)
    j = pl.program_id(3)

    @pl.when(j == 0)
    def _():
        m_scr[...] = jnp.full(m_scr.shape, -1e30, F32)
        l_scr[...] = jnp.zeros(l_scr.shape, F32)
        acc_scr[...] = jnp.zeros(acc_scr.shape, F32)

    def step(masked):
        q = q_ref[...]
        k = k_ref[...]
        v = v_ref[...]
        for hh in range(2):
            s = _dot_nt(_head_select(q, hh), k)
            logits = s + (ccol_ref[0, :, hh:hh + 1] - crow_ref[0, hh:hh + 1, :])
            if masked:
                qpos = lax.broadcasted_iota(jnp.int32, logits.shape, 0)
                kpos = lax.broadcasted_iota(jnp.int32, logits.shape, 1)
                logits = jnp.where(kpos <= qpos, logits, -1e30)
            m_prev = m_scr[hh]
            m_new = jnp.maximum(m_prev, jnp.max(logits, axis=1, keepdims=True))
            alpha = jnp.exp(m_prev - m_new)
            p = jnp.exp(logits - m_new)
            l_scr[hh] = alpha * l_scr[hh] + jnp.sum(p, axis=1, keepdims=True)
            acc_scr[hh] = alpha * acc_scr[hh] + jnp.dot(p.astype(BF16), v, preferred_element_type=F32)
            m_scr[hh] = m_new

    @pl.when(j < i)
    def _():
        step(False)

    @pl.when(j == i)
    def _():
        step(True)
        lane = lax.broadcasted_iota(jnp.int32, o_ref.shape, 1)
        o_ref[...] = jnp.where(lane < HEAD_DIM, acc_scr[0] / l_scr[0], acc_scr[1] / l_scr[1])


def _fox_attn(qkv, ccol, crow, bsz, seq, blk):
    nb = seq // blk
    t = bsz * seq
    return pl.pallas_call(
        _fox_kernel,
        grid=(bsz, 2, nb, nb),
        in_specs=[pl.BlockSpec((blk, LANE), lambda b, p, i, j: (b * nb + i, p)),
                  pl.BlockSpec((blk, LANE), lambda b, p, i, j: (b * nb + jnp.minimum(j, i), 2 + p)),
                  pl.BlockSpec((blk, LANE), lambda b, p, i, j: (b * nb + jnp.minimum(j, i), 4 + p)),
                  pl.BlockSpec((1, blk, 2), lambda b, p, i, j: (p, b * nb + i, 0)),
                  pl.BlockSpec((1, 8, blk), lambda b, p, i, j: (b * 2 + p, 0, jnp.minimum(j, i)))],
        out_specs=pl.BlockSpec((blk, LANE), lambda b, p, i, j: (b * nb + i, p)),
        out_shape=jax.ShapeDtypeStruct((t, BW), F32),
        scratch_shapes=[pltpu.VMEM((2, blk, 1), F32), pltpu.VMEM((2, blk, 1), F32),
                        pltpu.VMEM((2, blk, LANE), F32)],
        compiler_params=_cp(("parallel", "parallel", "parallel", "arbitrary")),
        name="fox_attention",
    )(qkv, qkv, qkv, ccol, crow)


def _sb_kernel(q_ref, k_ref, v_ref, u_ref, o_ref, suf_scr, acc_scr):
    i = pl.program_id(2)
    jj = pl.program_id(3)

    def step(masked):
        q = q_ref[...]
        k = k_ref[...]
        v = v_ref[...]
        u = u_ref[...]
        for hh in range(2):
            z = _dot_nt(_head_select(q, hh), k)
            sp = _softplus(z)
            lr = -sp
            if masked:
                qpos = lax.broadcasted_iota(jnp.int32, z.shape, 0)
                kpos = lax.broadcasted_iota(jnp.int32, z.shape, 1)
                mask = kpos < qpos
                lr = jnp.where(mask, lr, 0.0)
            hi = lr.astype(BF16)
            lo = (lr - hi.astype(F32)).astype(BF16)
            cum = (jnp.dot(hi, u, preferred_element_type=F32)
                   + jnp.dot(lo, u, preferred_element_type=F32))
            w = jnp.exp((z - sp) + (cum + suf_scr[hh]))
            if masked:
                w = jnp.where(mask, w, 0.0)
            acc_scr[hh] = acc_scr[hh] + jnp.dot(w.astype(BF16), v, preferred_element_type=F32)
            suf_scr[hh] = suf_scr[hh] + jnp.sum(lr, axis=1, keepdims=True)

    @pl.when(jj == 0)
    def _():
        suf_scr[...] = jnp.zeros(suf_scr.shape, F32)
        acc_scr[...] = jnp.zeros(acc_scr.shape, F32)
        step(True)

    @pl.when(jnp.logical_and(jj > 0, jj <= i))
    def _():
        step(False)

    @pl.when(jj == i)
    def _():
        lane = lax.broadcasted_iota(jnp.int32, o_ref.shape, 1)
        o_ref[...] = jnp.where(lane < HEAD_DIM, acc_scr[0], acc_scr[1])


def _sb_attn(qkv, umat, bsz, seq, blk):
    nb = seq // blk
    t = bsz * seq
    kv = lambda b, i, jj: b * nb + jnp.maximum(i - jj, 0)
    return pl.pallas_call(
        _sb_kernel,
        grid=(bsz, 2, nb, nb),
        in_specs=[pl.BlockSpec((blk, LANE), lambda b, p, i, jj: (b * nb + i, p)),
                  pl.BlockSpec((blk, LANE), lambda b, p, i, jj: (kv(b, i, jj), 2 + p)),
                  pl.BlockSpec((blk, LANE), lambda b, p, i, jj: (kv(b, i, jj), 4 + p)),
                  pl.BlockSpec((blk, blk), lambda b, p, i, jj: (0, 0))],
        out_specs=pl.BlockSpec((blk, LANE), lambda b, p, i, jj: (b * nb + i, p)),
        out_shape=jax.ShapeDtypeStruct((t, BW), F32),
        scratch_shapes=[pltpu.VMEM((2, blk, 1), F32), pltpu.VMEM((2, blk, LANE), F32)],
        compiler_params=_cp(("parallel", "parallel", "parallel", "arbitrary")),
        name="sb_attention",
    )(qkv, qkv, qkv, umat)


def _rwkv_kernel(rw_ref, mu_ref, w0_ref, w2_ref, a0_ref, a2_ref, g2_ref, kk_ref, ka_ref, rk_ref,
                 lg_ref, lb_ref, e_ref, tri_ref, msk_ref, hm_ref, o_ref, ht_scr, last_scr, *, chunk):
    L = chunk

    @pl.when(pl.program_id(1) == 0)
    def _():
        ht_scr[...] = jnp.zeros(ht_scr.shape, F32)
        last_scr[...] = jnp.zeros(last_scr.shape, F32)

    x = rw_ref[...]
    row = lax.broadcasted_iota(jnp.int32, x.shape, 0)
    prev = jnp.where(row == 0, last_scr[0:1, :], pltpu.roll(x, 1, axis=0))
    last_scr[0:1, :] = x[L - 1:L, :]
    xs = x + (prev - x) * mu_ref[...]

    r = xs[:, 0:256]
    k = xs[:, 256:512]
    v = xs[:, 512:768]
    wa = xs[:, 768:896]
    g1 = xs[:, 896:1024]
    wpre = w0_ref[...] + _dot_hi(jnp.tanh(wa), w2_ref[...])
    w = -_softplus(-wpre) - 0.5
    ld = -jnp.exp(w)
    a = jax.nn.sigmoid(a0_ref[...] + _dot_hi(wa, a2_ref[...]))
    g = _dot_hi(jax.nn.sigmoid(g1), g2_ref[...])
    e = e_ref[...]
    kk = k * kk_ref[...]
    kap = kk / jnp.maximum(jnp.sqrt(_dot_hi(kk * kk, e)), 1e-12)
    k2 = k * (1.0 + (a - 1.0) * ka_ref[...])
    beta = kap * a

    cs = _dot_hi(tri_ref[...], ld)
    gam = jnp.exp(cs)
    ginv = jnp.exp(-cs)
    g_last = gam[L - 1:L, :]

    def stack(t):
        return jnp.concatenate([t * hm_ref[h] for h in range(N_HEADS)], axis=0).astype(BF16)

    kts = stack(kap * jnp.exp(cs - ld))
    rts = stack(r * gam)
    khs = stack(k2 * ginv)
    bhs = stack(beta * ginv)
    vs = stack(v)
    strict = msk_ref[0]
    incl = msk_ref[1]
    m = _dot_nt(kts, bhs) * strict
    akk = _dot_nt(kts, khs) * strict
    ark = _dot_nt(rts, khs) * incl
    arb = _dot_nt(rts, bhs) * incl

    md = m * msk_ref[2]
    p2 = _dot(md, md)
    p4 = _dot(p2, p2)
    p8 = _dot(p4, p4)
    xinv = msk_ref[5] - md
    xinv = xinv + _dot(xinv, p2)
    xinv = xinv + _dot(xinv, p4)
    xinv = xinv + _dot(xinv, p8)
    xinv = xinv - _dot(_dot(xinv, m * msk_ref[3]), xinv)
    xinv = xinv - _dot(_dot(xinv, m * msk_ref[4]), xinv)

    ht = ht_scr[...]
    wmat = _dot_nt(kts, ht) + _dot(akk, vs)
    us = _dot(xinv, wmat)
    ys = _dot_nt(rts, ht) + _dot(ark, vs) - _dot(arb, us)
    y = ys[0:L] + ys[L:2 * L] + ys[2 * L:3 * L] + ys[3 * L:4 * L]

    lhs = jnp.concatenate([vs, us.astype(BF16)], axis=0)
    rhs = jnp.concatenate([stack(k2 * ginv * g_last), stack(-(beta * ginv * g_last))], axis=0)
    ht_scr[...] = ht * g_last + _dot_tn(lhs, rhs) * e

    ym = _dot_hi(y, e) * (1.0 / HEAD_DIM)
    d = y - ym
    yv = _dot_hi(d * d, e) * (1.0 / HEAD_DIM)
    yn = d * lax.rsqrt(yv + GN_EPS) * lg_ref[...] + lb_ref[...]
    bonus = _dot_hi(r * k2 * rk_ref[...], e) * v
    o_ref[...] = (yn + bonus) * g


def _rwkv_consts(chunk):
    n = N_HEADS * chunk
    idx = jnp.arange(n)
    rh, rt = idx // chunk, idx % chunk
    same = rh[:, None] == rh[None, :]
    blk = lambda s: same & ((rt[:, None] // s) == (rt[None, :] // s))
    masks = jnp.stack([
        same & (rt[:, None] > rt[None, :]),
        same & (rt[:, None] >= rt[None, :]),
        blk(16),
        blk(32) & ~blk(16),
        blk(64) & ~blk(32),
        jnp.eye(n, dtype=bool),
    ]).astype(F32)
    lane_head = jnp.arange(BW) // HEAD_DIM
    e = (lane_head[:, None] == lane_head[None, :]).astype(F32)
    hm = (jnp.arange(N_HEADS)[:, None] == lane_head[None, :]).astype(F32).reshape(N_HEADS, 1, BW)
    tri = (jnp.arange(chunk)[:, None] >= jnp.arange(chunk)[None, :]).astype(F32)
    return e, tri, masks, hm


def _rwkv(rw, params, consts, bsz, seq, chunk):
    nc = seq // chunk
    t = bsz * seq
    n = N_HEADS * chunk
    mu, w0, w2p, a0, a2p, g2, k_k, k_a, r_k, lg, lb = params
    e, tri, masks, hm = consts
    c2 = lambda shape: pl.BlockSpec(shape, lambda b, c: (0, 0))
    c3 = lambda shape: pl.BlockSpec(shape, lambda b, c: (0, 0, 0))
    return pl.pallas_call(
        functools.partial(_rwkv_kernel, chunk=chunk),
        grid=(bsz, nc),
        in_specs=[pl.BlockSpec((chunk, RWKV_PROJ), lambda b, c: (b * nc + c, 0)),
                  c2((1, RWKV_PROJ)), c2((1, BW)), c2((LANE, BW)), c2((1, BW)), c2((LANE, BW)),
                  c2((LANE, BW)), c2((1, BW)), c2((1, BW)), c2((1, BW)), c2((1, BW)), c2((1, BW)),
                  c2((BW, BW)), c2((chunk, chunk)), c3((6, n, n)), c3((N_HEADS, 1, BW))],
        out_specs=pl.BlockSpec((chunk, BW), lambda b, c: (b * nc + c, 0)),
        out_shape=jax.ShapeDtypeStruct((t, BW), F32),
        scratch_shapes=[pltpu.VMEM((BW, BW), F32), pltpu.VMEM((8, RWKV_PROJ), F32)],
        compiler_params=_cp(("parallel", "arbitrary")),
        name="rwkv7_mix",
    )(rw, mu, w0, w2p, a0, a2p, g2, k_k, k_a, r_k, lg, lb, e, tri, masks, hm)


def _merge_kernel(h_ref, y0_ref, y1_ref, y2_ref, y3_ref, gate_ref, wb_ref, wo_ref, g_ref, b_ref, o_ref):
    merged = None
    for n, y_ref in enumerate((y0_ref, y1_ref, y2_ref, y3_ref)):
        up = jnp.dot(y_ref[...].astype(BF16), wb_ref[n], preferred_element_type=F32)
        term = gate_ref[:, n * D_MODEL:(n + 1) * D_MODEL].astype(F32) * up
        merged = term if merged is None else merged + term
    z = ALPHA * h_ref[...] + jnp.dot(merged.astype(BF16), wo_ref[...], preferred_element_type=F32)
    o_ref[...] = _layer_norm(z, g_ref[...], b_ref[...])


def _merge(h, y_s5, y_fox, y_rwkv, y_sb, gates, wb, wo, g, b, bsz, seq, tm):
    ns = seq // tm
    t = bsz * seq
    row = lambda bi, i: (bi * ns + i, 0)
    return pl.pallas_call(
        _merge_kernel,
        grid=(bsz, ns),
        in_specs=[pl.BlockSpec((tm, D_MODEL), row),
                  pl.BlockSpec((tm, BW), lambda bi, i: (i, bi)),
                  pl.BlockSpec((tm, BW), row), pl.BlockSpec((tm, BW), row), pl.BlockSpec((tm, BW), row),
                  pl.BlockSpec((tm, N_BRANCHES * D_MODEL), row),
                  pl.BlockSpec((N_BRANCHES, BW, D_MODEL), lambda bi, i: (0, 0, 0)),
                  pl.BlockSpec((D_MODEL, D_MODEL), lambda bi, i: (0, 0)),
                  pl.BlockSpec((1, D_MODEL), lambda bi, i: (0, 0)),
                  pl.BlockSpec((1, D_MODEL), lambda bi, i: (0, 0))],
        out_specs=pl.BlockSpec((tm, D_MODEL), row),
        out_shape=jax.ShapeDtypeStruct((t, D_MODEL), F32),
        compiler_params=_cp(("parallel", "parallel")),
        name="merge_ln1",
    )(h, y_s5, y_fox, y_rwkv, y_sb, gates, wb, wo, g, b)


def _mlp_kernel(h_ref, p_ref, w1_ref, w2_ref, pw_ref, pg_ref, g_ref, b_ref, o_ref, *, ff_chunk):
    h = h_ref[...]
    hb = h.astype(BF16)
    ffn = None
    for c in range(D_FF // ff_chunk):
        sl = slice(c * ff_chunk, (c + 1) * ff_chunk)
        hid = jnp.maximum(jnp.dot(hb, w1_ref[:, sl], preferred_element_type=F32), 0.0)
        part = jnp.dot((hid * hid).astype(BF16), w2_ref[sl, :], preferred_element_type=F32)
        ffn = part if ffn is None else ffn + part
    ple = (jax.nn.sigmoid(jnp.dot(hb, pg_ref[...], preferred_element_type=F32))
           * jnp.dot(p_ref[...].astype(BF16), pw_ref[...], preferred_element_type=F32))
    o_ref[...] = _layer_norm(ALPHA * h + ffn + ple, g_ref[...], b_ref[...])


def _mlp(h, p, w1, w2, pw, pg, g, b, tm):
    t = h.shape[0]
    full = lambda shape: pl.BlockSpec(shape, lambda i: (0, 0))
    return pl.pallas_call(
        functools.partial(_mlp_kernel, ff_chunk=1024),
        grid=(t // tm,),
        in_specs=[pl.BlockSpec((tm, D_MODEL), lambda i: (i, 0)),
                  pl.BlockSpec((tm, PLE_DIM), lambda i: (i, 0)),
                  full((D_MODEL, D_FF)), full((D_FF, D_MODEL)), full((PLE_DIM, D_MODEL)),
                  full((D_MODEL, D_MODEL)), full((1, D_MODEL)), full((1, D_MODEL))],
        out_specs=pl.BlockSpec((tm, D_MODEL), lambda i: (i, 0)),
        out_shape=jax.ShapeDtypeStruct((t, D_MODEL), F32),
        compiler_params=_cp(("parallel",)),
        name="mlp_ple_ln2",
    )(h, p, w1, w2, pw, pg, g, b)


def _pad_rows(m, rows, at):
    out = jnp.zeros((rows, m.shape[1]), m.dtype)
    return out.at[at:at + m.shape[0]].set(m)


def kernel(x, p, w_in, s5_lambda_re, s5_lambda_im, s5_log_dt, s5_b_re, s5_b_im, s5_c_re, s5_c_im, s5_d, s5_glu_w, s5_glu_b, fox_f_bias, rwkv_mu, rwkv_w0, rwkv_w2, rwkv_a0, rwkv_a2, rwkv_g2, rwkv_k_k, rwkv_k_a, rwkv_r_k, rwkv_lnx_g, rwkv_lnx_b, w_branch, w_out, ln1_g, ln1_b, mlp_w1, mlp_w2, ple_w, ple_gate_w, ln2_g, ln2_b):
    bsz, seq, _ = x.shape
    t = bsz * seq
    tm = min(512, seq)
    blk = min(256, seq)
    ts = min(128, seq)
    chunk = RWKV_CHUNK
    scale = HEAD_DIM ** -0.5

    rwkv_consts = _rwkv_consts(chunk)
    tri_lane = (jnp.arange(LANE)[:, None] <= jnp.arange(LANE)[None, :]).astype(F32)
    u_strict = (jnp.arange(blk)[:, None] > jnp.arange(blk)[None, :]).astype(BF16)
    row2 = lambda a: a.reshape(1, -1)

    h = x.reshape(t, D_MODEL)
    for i in range(DEPTH):
        w = w_in[i]
        w_main = jnp.concatenate([
            w[:, 0:256], w[:, 256:512] * scale, w[:, 512:1024],
            w[:, 1028:2052],
            w[:, 2052:2308] * scale, w[:, 2308:2820]], axis=1).astype(BF16)
        w_ff = _pad_rows(w[:, 1024:1028].T, 8, 0).astype(BF16)
        w_gates = w[:, 2820:].astype(BF16)

        u_sb, fqkv, rw, sqkv, fft = _proj(h, w_main, w_ff, bsz, seq, tm)
        gates = _gates(h, w_gates, tm)

        bbar, cmat, lam = _s5_params(s5_lambda_re[i], s5_lambda_im[i], s5_log_dt[i], s5_b_re[i],
                                     s5_b_im[i], s5_c_re[i], s5_c_im[i], bsz)
        y_s5 = _s5(u_sb.reshape(seq * bsz, BW), bbar, cmat, lam, row2(s5_d[i]),
                   s5_glu_w[i].astype(BF16), row2(s5_glu_b[i]), bsz, seq, ts)
        y_s5 = y_s5.reshape(seq, bsz * BW)

        bias8 = _pad_rows(fox_f_bias[i].reshape(N_HEADS, 1), 8, 0)
        c_row = _fox_c(fft, bias8, tri_lane)
        crow = jnp.pad(c_row[:, :N_HEADS].reshape(bsz * 2, 2, seq), ((0, 0), (0, 6), (0, 0)))
        ccol = jnp.transpose(c_row[:, :N_HEADS], (1, 0, 2)).reshape(2, 2, t)
        ccol = jnp.transpose(ccol, (0, 2, 1))
        y_fox = _fox_attn(fqkv, ccol, crow, bsz, seq, blk)

        rparams = (row2(rwkv_mu[i]), row2(rwkv_w0[i]), _pad_rows(rwkv_w2[i], LANE, 0),
                   row2(rwkv_a0[i]), _pad_rows(rwkv_a2[i], LANE, 64), rwkv_g2[i],
                   row2(rwkv_k_k[i]), row2(rwkv_k_a[i]), row2(rwkv_r_k[i]),
                   row2(rwkv_lnx_g[i]), row2(rwkv_lnx_b[i]))
        y_rwkv = _rwkv(rw, rparams, rwkv_consts, bsz, seq, chunk)

        y_sb = _sb_attn(sqkv, u_strict, bsz, seq, blk)

        h = _merge(h, y_s5, y_fox, y_rwkv, y_sb, gates, w_branch[i].astype(BF16),
                   w_out[i].astype(BF16), row2(ln1_g[i]), row2(ln1_b[i]), bsz, seq, tm)
        h = _mlp(h, p[i].reshape(t, PLE_DIM), mlp_w1[i].astype(BF16), mlp_w2[i].astype(BF16),
                 ple_w[i].astype(BF16), ple_gate_w[i].astype(BF16), row2(ln2_g[i]), row2(ln2_b[i]), tm)
    return h.reshape(bsz, seq, D_MODEL)
```

```python
import functools
import math

import jax
import jax.numpy as jnp
from jax import lax
from jax.experimental import pallas as pl
from jax.experimental.pallas import tpu as pltpu

F32 = jnp.float32
BF16 = jnp.bfloat16
HI = lax.Precision.HIGHEST

D_MODEL = 1024
PLE_DIM = 256
N_BRANCHES = 4
BW = 256
HEAD_DIM = 64
N_HEADS = 4
S5_GROUP_CH = 16
S5_GROUPS = 16
S5_STATE = 64
S5_N = S5_GROUPS * S5_STATE
RWKV_PROJ = 1024
D_FF = 4096
LN_EPS = 1e-5
GN_EPS = 64e-5
DEPTH = 2
ALPHA = (2 * DEPTH) ** 0.25
LANE = 128
RWKV_CHUNK = 64
ATT_TILE = 256
VMEM_LIMIT = 56 * 1024 * 1024


def _cp(sem, vmem=VMEM_LIMIT):
    return pltpu.CompilerParams(dimension_semantics=sem, vmem_limit_bytes=vmem)


def _dot(a, b):
    return jnp.dot(a.astype(BF16), b.astype(BF16), preferred_element_type=F32)


def _dot_hi(a, b):
    return jnp.dot(a, b, precision=HI, preferred_element_type=F32)


def _dot_nt(a, b):
    return lax.dot_general(a.astype(BF16), b.astype(BF16), (((1,), (1,)), ((), ())),
                           preferred_element_type=F32)


def _dot_tn(a, b):
    return lax.dot_general(a.astype(BF16), b.astype(BF16), (((0,), (0,)), ((), ())),
                           preferred_element_type=F32)


def _softplus(x):
    return jnp.maximum(x, 0.0) + jnp.log1p(jnp.exp(-jnp.abs(x)))


def _gelu_tanh(x):
    return 0.5 * x * (1.0 + jnp.tanh(math.sqrt(2.0 / math.pi) * (x + 0.044715 * (x * x * x))))


def _layer_norm(z, g, b):
    mu = jnp.mean(z, axis=-1, keepdims=True)
    d = z - mu
    var = jnp.mean(d * d, axis=-1, keepdims=True)
    return d * lax.rsqrt(var + LN_EPS) * g + b


def _proj_kernel(h_ref, wm_ref, wvt_ref, wf_ref, u_ref, fqk_ref, rw_ref, sqk_ref, fvt_ref, svt_ref,
                 fft_ref, *, tm):
    hb = h_ref[...].astype(BF16)
    u_ref[...] = jnp.dot(hb, wm_ref[:, 0:256], preferred_element_type=F32)
    fqk_ref[...] = jnp.dot(hb, wm_ref[:, 256:768], preferred_element_type=F32).astype(BF16)
    rw_ref[...] = jnp.dot(hb, wm_ref[:, 768:1792], preferred_element_type=F32)
    sqk_ref[...] = jnp.dot(hb, wm_ref[:, 1792:2304], preferred_element_type=F32).astype(BF16)
    vt = _dot_nt(wvt_ref[...], hb).astype(BF16)
    for jj in range(tm // ATT_TILE):
        sl = slice(jj * ATT_TILE, (jj + 1) * ATT_TILE)
        fvt_ref[0, jj] = vt[0:BW, sl]
        svt_ref[0, jj] = vt[BW:2 * BW, sl]
    fft_ref[0] = _dot_nt(wf_ref[...], hb)


def _proj(h, wm, wvt, wf, bsz, seq, tm):
    ns = seq // tm
    t = bsz * seq
    row = lambda b, i: (b * ns + i, 0)
    vt_spec = pl.BlockSpec((1, tm // ATT_TILE, BW, ATT_TILE), lambda b, i: (b, i, 0, 0))
    vt_shape = jax.ShapeDtypeStruct((bsz, seq // ATT_TILE, BW, ATT_TILE), BF16)
    return pl.pallas_call(
        functools.partial(_proj_kernel, tm=tm),
        grid=(bsz, ns),
        in_specs=[pl.BlockSpec((tm, D_MODEL), row),
                  pl.BlockSpec((D_MODEL, 2304), lambda b, i: (0, 0)),
                  pl.BlockSpec((2 * BW, D_MODEL), lambda b, i: (0, 0)),
                  pl.BlockSpec((8, D_MODEL), lambda b, i: (0, 0))],
        out_specs=[pl.BlockSpec((tm, BW), lambda b, i: (i, b)),
                   pl.BlockSpec((tm, 2 * BW), row),
                   pl.BlockSpec((tm, RWKV_PROJ), row),
                   pl.BlockSpec((tm, 2 * BW), row),
                   vt_spec, vt_spec,
                   pl.BlockSpec((1, 8, tm), lambda b, i: (b, 0, i))],
        out_shape=[jax.ShapeDtypeStruct((seq, bsz * BW), F32),
                   jax.ShapeDtypeStruct((t, 2 * BW), BF16),
                   jax.ShapeDtypeStruct((t, RWKV_PROJ), F32),
                   jax.ShapeDtypeStruct((t, 2 * BW), BF16),
                   vt_shape, vt_shape,
                   jax.ShapeDtypeStruct((bsz, 8, seq), F32)],
        compiler_params=_cp(("parallel", "parallel")),
        name="in_proj",
    )(h, wm, wvt, wf)


def _gates_kernel(h_ref, wg_ref, o_ref):
    hb = h_ref[...].astype(BF16)
    for c in range(N_BRANCHES):
        sl = slice(c * D_MODEL, (c + 1) * D_MODEL)
        o_ref[:, sl] = jax.nn.sigmoid(
            jnp.dot(hb, wg_ref[:, sl], preferred_element_type=F32)).astype(BF16)


def _gates(h, wg, tm):
    t = h.shape[0]
    return pl.pallas_call(
        _gates_kernel,
        grid=(t // tm,),
        in_specs=[pl.BlockSpec((tm, D_MODEL), lambda i: (i, 0)),
                  pl.BlockSpec((D_MODEL, N_BRANCHES * D_MODEL), lambda i: (0, 0))],
        out_specs=pl.BlockSpec((tm, N_BRANCHES * D_MODEL), lambda i: (i, 0)),
        out_shape=jax.ShapeDtypeStruct((t, N_BRANCHES * D_MODEL), BF16),
        compiler_params=_cp(("parallel",)),
        name="gate_proj",
    )(h, wg)


def _s5_kernel(u_ref, bbar_ref, cmat_ref, lam_ref, d_ref, gw_ref, gb_ref, y_ref, xs_scr, st_scr,
               *, ts, bsz):
    @pl.when(pl.program_id(0) == 0)
    def _():
        st_scr[...] = jnp.zeros(st_scr.shape, F32)

    u = u_ref[...]
    xs_scr[...] = jnp.dot(u.astype(BF16), bbar_ref[...], preferred_element_type=F32)
    lr = lam_ref[:, 0:S5_N]
    li = lam_ref[:, S5_N:2 * S5_N]

    def body(t, carry):
        xr, xi = carry
        r0 = pl.multiple_of(t * bsz, bsz)
        br = xs_scr[pl.ds(r0, bsz), 0:S5_N]
        bi = xs_scr[pl.ds(r0, bsz), S5_N:2 * S5_N]
        nr = lr * xr - li * xi + br
        ni = lr * xi + li * xr + bi
        xs_scr[pl.ds(r0, bsz), 0:S5_N] = nr
        xs_scr[pl.ds(r0, bsz), S5_N:2 * S5_N] = ni
        return nr, ni

    xr, xi = lax.fori_loop(0, ts, body, (st_scr[:, 0:S5_N], st_scr[:, S5_N:2 * S5_N]), unroll=4)
    st_scr[:, 0:S5_N] = xr
    st_scr[:, S5_N:2 * S5_N] = xi

    y = jnp.dot(xs_scr[...].astype(BF16), cmat_ref[...], preferred_element_type=F32) + d_ref[...] * u
    y = _gelu_tanh(y)
    y_ref[...] = y * jax.nn.sigmoid(_dot(y, gw_ref[...]) + gb_ref[...])


def _s5(u_sb, bbar, cmat, lam, dvec, gw, gb, bsz, seq, ts):
    rows = ts * bsz
    full = lambda shape: pl.BlockSpec(shape, lambda i: (0, 0))
    return pl.pallas_call(
        functools.partial(_s5_kernel, ts=ts, bsz=bsz),
        grid=(seq // ts,),
        in_specs=[pl.BlockSpec((rows, BW), lambda i: (i, 0)),
                  full((BW, 2 * S5_N)), full((2 * S5_N, BW)), full((bsz, 2 * S5_N)),
                  full((1, BW)), full((BW, BW)), full((1, BW))],
        out_specs=pl.BlockSpec((rows, BW), lambda i: (i, 0)),
        out_shape=jax.ShapeDtypeStruct((seq * bsz, BW), F32),
        scratch_shapes=[pltpu.VMEM((rows, 2 * S5_N), F32), pltpu.VMEM((bsz, 2 * S5_N), F32)],
        compiler_params=_cp(("arbitrary",)),
        name="s5_mixer",
    )(u_sb, bbar, cmat, lam, dvec, gw, gb)


def _s5_params(lam_re, lam_im, log_dt, b_re, b_im, c_re, c_im, bsz):
    dt = jnp.exp(log_dt)[:, None]
    mag = jnp.exp(lam_re * dt)
    ang = lam_im * dt
    lb_re = mag * jnp.cos(ang)
    lb_im = mag * jnp.sin(ang)
    den = jnp.square(lam_re) + jnp.square(lam_im)
    nr = lb_re - 1.0
    f_re = (nr * lam_re + lb_im * lam_im) / den
    f_im = (lb_im * lam_re - nr * lam_im) / den
    bb_re = f_re[..., None] * b_re - f_im[..., None] * b_im
    bb_im = f_re[..., None] * b_im + f_im[..., None] * b_re
    eye = jnp.eye(S5_GROUPS, dtype=F32)
    bd_in = lambda m: jnp.einsum('gph,gk->ghkp', m, eye).reshape(BW, S5_N)
    bd_out = lambda m: jnp.einsum('ghp,gk->gpkh', m, eye).reshape(S5_N, BW)
    bbar = jnp.concatenate([bd_in(bb_re), bd_in(bb_im)], axis=1).astype(BF16)
    cmat = jnp.concatenate([bd_out(c_re), -bd_out(c_im)], axis=0).astype(BF16)
    lam = jnp.concatenate([lb_re.reshape(1, S5_N), lb_im.reshape(1, S5_N)], axis=1)
    return bbar, cmat, jnp.broadcast_to(lam, (bsz, 2 * S5_N))


def _foxc_kernel(ff_ref, bias_ref, tri_ref, c_ref, *, n_chunks):
    carry = jnp.zeros((8, 1), F32)
    for ch in range(n_chunks):
        sl = slice(ch * LANE, (ch + 1) * LANE)
        x = ff_ref[0, :, sl] + bias_ref[...]
        lf = jnp.minimum(x, 0.0) - jnp.log1p(jnp.exp(-jnp.abs(x)))
        cs = _dot_hi(lf, tri_ref[...]) + carry
        c_ref[0, :, sl] = cs
        carry = cs[:, LANE - 1:LANE]


def _fox_c(fft, bias8, tri):
    bsz, _, seq = fft.shape
    return pl.pallas_call(
        functools.partial(_foxc_kernel, n_chunks=seq // LANE),
        grid=(bsz,),
        in_specs=[pl.BlockSpec((1, 8, seq), lambda b: (b, 0, 0)),
                  pl.BlockSpec((8, 1), lambda b: (0, 0)),
                  pl.BlockSpec((LANE, LANE), lambda b: (0, 0))],
        out_specs=pl.BlockSpec((1, 8, seq), lambda b: (b, 0, 0)),
        out_shape=jax.ShapeDtypeStruct((bsz, 8, seq), F32),
        compiler_params=_cp(("parallel",)),
        name="fox_cumlogf",
    )(fft, bias8, tri)


def _head_select(x, hh):
    lane = lax.broadcasted_iota(jnp.int32, x.shape, 1)
    sel = (lane < HEAD_DIM) if hh == 0 else (lane >= HEAD_DIM)
    return jnp.where(sel, x, jnp.zeros_like(x))


def _pair_cols(h):
    return slice((h // 2) * LANE, (h // 2 + 1) * LANE)


def _head_rows(h):
    return slice(h * HEAD_DIM, (h + 1) * HEAD_DIM)


def _split_heads(q):
    return [_head_select(q[:, _pair_cols(h)], h % 2) for h in range(N_HEADS)]


def _tile_positions():
    kpos = lax.broadcasted_iota(jnp.int32, (ATT_TILE, ATT_TILE), 0)
    qpos = lax.broadcasted_iota(jnp.int32, (ATT_TILE, ATT_TILE), 1)
    return kpos, qpos


def _fox_kernel(q_ref, k_ref, vt_ref, ck_ref, cq_ref, o_ref, acc_scr):
    i = pl.program_id(1)
    qh = _split_heads(q_ref[...])
    cq = [cq_ref[0, h:h + 1, :] for h in range(N_HEADS)]
    acc_scr[...] = jnp.zeros(acc_scr.shape, F32)

    def tile(j, carry, masked):
        k0 = pl.multiple_of(j * ATT_TILE, ATT_TILE)
        ks = k_ref[pl.ds(k0, ATT_TILE), :]
        vt = vt_ref[0, j]
        s = [_dot_nt(ks[:, _pair_cols(h)], qh[h]) for h in range(N_HEADS)]
        new = []
        for h in range(N_HEADS):
            m, l = carry[h]
            t1 = s[h] - ck_ref[pl.ds(k0, ATT_TILE), h:h + 1]
            if masked:
                kpos, qpos = _tile_positions()
                t1 = jnp.where(kpos <= qpos, t1, -1e30)
            m_new = jnp.maximum(m, jnp.max(t1, axis=0, keepdims=True) + cq[h])
            alpha = jnp.exp(m - m_new)
            p = jnp.exp(t1 - (m_new - cq[h]))
            l = alpha * l + jnp.sum(p, axis=0, keepdims=True)
            acc_scr[h] = alpha * acc_scr[h] + jnp.dot(vt[_head_rows(h), :], p.astype(BF16),
                                                      preferred_element_type=F32)
            new.append((m_new, l))
        return tuple(new)

    init = tuple((jnp.full((1, ATT_TILE), -1e30, F32), jnp.zeros((1, ATT_TILE), F32))
                 for _ in range(N_HEADS))
    carry = lax.fori_loop(0, i, lambda j, c: tile(j, c, False), init)
    carry = tile(i, carry, True)
    out_t = jnp.concatenate([acc_scr[h] / carry[h][1] for h in range(N_HEADS)], axis=0)
    o_ref[...] = out_t.T


def _att_specs(seq, nq):
    return [pl.BlockSpec((ATT_TILE, BW), lambda b, i: (b * nq + i, 0)),
            pl.BlockSpec((seq, BW), lambda b, i: (b, 1)),
            pl.BlockSpec((1, seq // ATT_TILE, BW, ATT_TILE), lambda b, i: (b, 0, 0, 0))]


def _fox_attn(qk, vt, ccol, crow, bsz, seq):
    nq = seq // ATT_TILE
    return pl.pallas_call(
        _fox_kernel,
        grid=(bsz, nq),
        in_specs=_att_specs(seq, nq) + [
            pl.BlockSpec((seq, N_HEADS), lambda b, i: (b, 0)),
            pl.BlockSpec((1, 8, ATT_TILE), lambda b, i: (b, 0, i))],
        out_specs=pl.BlockSpec((ATT_TILE, BW), lambda b, i: (b * nq + i, 0)),
        out_shape=jax.ShapeDtypeStruct((bsz * seq, BW), F32),
        scratch_shapes=[pltpu.VMEM((N_HEADS, HEAD_DIM, ATT_TILE), F32)],
        compiler_params=_cp(("parallel", "parallel")),
        name="fox_attention",
    )(qk, qk, vt, ccol, crow)


def _sb_kernel(q_ref, k_ref, vt_ref, w_ref, o_ref, acc_scr):
    i = pl.program_id(1)
    qh = _split_heads(q_ref[...])
    wmat = w_ref[...]
    acc_scr[...] = jnp.zeros(acc_scr.shape, F32)

    def tile(j, carry, masked):
        k0 = pl.multiple_of(j * ATT_TILE, ATT_TILE)
        ks = k_ref[pl.ds(k0, ATT_TILE), :]
        vt = vt_ref[0, j]
        z = [_dot_nt(ks[:, _pair_cols(h)], qh[h]) for h in range(N_HEADS)]
        new = []
        for h in range(N_HEADS):
            zz = z[h]
            sp = jnp.maximum(zz, 0.0) + jnp.log(1.0 + jnp.exp(-jnp.abs(zz)))
            if masked:
                kpos, qpos = _tile_positions()
                mask = kpos < qpos
                sp = jnp.where(mask, sp, 0.0)
            later = jnp.dot(wmat, sp.astype(BF16), preferred_element_type=F32)
            w = jnp.exp((zz - sp) - later - carry[h])
            if masked:
                w = jnp.where(mask, w, 0.0)
            acc_scr[h] = acc_scr[h] + jnp.dot(vt[_head_rows(h), :], w.astype(BF16),
                                              preferred_element_type=F32)
            new.append(carry[h] + jnp.sum(sp, axis=0, keepdims=True))
        return tuple(new)

    carry = tile(i, tuple(jnp.zeros((1, ATT_TILE), F32) for _ in range(N_HEADS)), True)
    lax.fori_loop(0, i, lambda it, c: tile(i - 1 - it, c, False), carry)
    o_ref[...] = jnp.concatenate([acc_scr[h] for h in range(N_HEADS)], axis=0).T


def _sb_attn(qk, vt, wmat, bsz, seq):
    nq = seq // ATT_TILE
    return pl.pallas_call(
        _sb_kernel,
        grid=(bsz, nq),
        in_specs=_att_specs(seq, nq) + [pl.BlockSpec((ATT_TILE, ATT_TILE), lambda b, i: (0, 0))],
        out_specs=pl.BlockSpec((ATT_TILE, BW), lambda b, i: (b * nq + i, 0)),
        out_shape=jax.ShapeDtypeStruct((bsz * seq, BW), F32),
        scratch_shapes=[pltpu.VMEM((N_HEADS, HEAD_DIM, ATT_TILE), F32)],
        compiler_params=_cp(("parallel", "parallel")),
        name="sb_attention",
    )(qk, qk, vt, wmat)


def _rwkv_kernel(rw_ref, mu_ref, w0_ref, w2_ref, a0_ref, a2_ref, g2_ref, kk_ref, ka_ref, rk_ref,
                 lg_ref, lb_ref, e_ref, tri_ref, msk_ref, hm_ref, o_ref, ht_scr, last_scr, *, chunk):
    L = chunk

    @pl.when(pl.program_id(1) == 0)
    def _():
        ht_scr[...] = jnp.zeros(ht_scr.shape, F32)
        last_scr[...] = jnp.zeros(last_scr.shape, F32)

    x = rw_ref[...]
    row = lax.broadcasted_iota(jnp.int32, x.shape, 0)
    prev = jnp.where(row == 0, last_scr[0:1, :], pltpu.roll(x, 1, axis=0))
    last_scr[0:1, :] = x[L - 1:L, :]
    xs = x + (prev - x) * mu_ref[...]

    r = xs[:, 0:256]
    k = xs[:, 256:512]
    v = xs[:, 512:768]
    wa = xs[:, 768:896]
    g1 = xs[:, 896:1024]
    wpre = w0_ref[...] + _dot_hi(jnp.tanh(wa), w2_ref[...])
    w = -_softplus(-wpre) - 0.5
    ld = -jnp.exp(w)
    a = jax.nn.sigmoid(a0_ref[...] + _dot_hi(wa, a2_ref[...]))
    g = _dot_hi(jax.nn.sigmoid(g1), g2_ref[...])
    e = e_ref[...]
    kk = k * kk_ref[...]
    kap = kk / jnp.maximum(jnp.sqrt(_dot_hi(kk * kk, e)), 1e-12)
    k2 = k * (1.0 + (a - 1.0) * ka_ref[...])
    beta = kap * a

    cs = _dot_hi(tri_ref[...], ld)
    gam = jnp.exp(cs)
    ginv = jnp.exp(-cs)
    g_last = gam[L - 1:L, :]

    def stack(t):
        return jnp.concatenate([t * hm_ref[h] for h in range(N_HEADS)], axis=0).astype(BF16)

    kts = stack(kap * jnp.exp(cs - ld))
    rts = stack(r * gam)
    khs = stack(k2 * ginv)
    bhs = stack(beta * ginv)
    vs = stack(v)
    strict = msk_ref[0]
    incl = msk_ref[1]
    m = _dot_nt(kts, bhs) * strict
    akk = _dot_nt(kts, khs) * strict
    ark = _dot_nt(rts, khs) * incl
    arb = _dot_nt(rts, bhs) * incl

    md = m * msk_ref[2]
    p2 = _dot(md, md)
    p4 = _dot(p2, p2)
    p8 = _dot(p4, p4)
    xinv = msk_ref[5] - md
    xinv = xinv + _dot(xinv, p2)
    xinv = xinv + _dot(xinv, p4)
    xinv = xinv + _dot(xinv, p8)
    xinv = xinv - _dot(_dot(xinv, m * msk_ref[3]), xinv)
    xinv = xinv - _dot(_dot(xinv, m * msk_ref[4]), xinv)

    ht = ht_scr[...]
    wmat = _dot_nt(kts, ht) + _dot(akk, vs)
    us = _dot(xinv, wmat)
    ys = _dot_nt(rts, ht) + _dot(ark, vs) - _dot(arb, us)
    y = ys[0:L] + ys[L:2 * L] + ys[2 * L:3 * L] + ys[3 * L:4 * L]

    lhs = jnp.concatenate([vs, us.astype(BF16)], axis=0)
    rhs = jnp.concatenate([stack(k2 * ginv * g_last), stack(-(beta * ginv * g_last))], axis=0)
    ht_scr[...] = ht * g_last + _dot_tn(lhs, rhs) * e

    ym = _dot_hi(y, e) * (1.0 / HEAD_DIM)
    d = y - ym
    yv = _dot_hi(d * d, e) * (1.0 / HEAD_DIM)
    yn = d * lax.rsqrt(yv + GN_EPS) * lg_ref[...] + lb_ref[...]
    bonus = _dot_hi(r * k2 * rk_ref[...], e) * v
    o_ref[...] = (yn + bonus) * g


def _rwkv_consts(chunk):
    n = N_HEADS * chunk
    idx = jnp.arange(n)
    rh, rt = idx // chunk, idx % chunk
    same = rh[:, None] == rh[None, :]
    blk = lambda s: same & ((rt[:, None] // s) == (rt[None, :] // s))
    masks = jnp.stack([
        same & (rt[:, None] > rt[None, :]),
        same & (rt[:, None] >= rt[None, :]),
        blk(16),
        blk(32) & ~blk(16),
        blk(64) & ~blk(32),
        jnp.eye(n, dtype=bool),
    ]).astype(F32)
    lane_head = jnp.arange(BW) // HEAD_DIM
    e = (lane_head[:, None] == lane_head[None, :]).astype(F32)
    hm = (jnp.arange(N_HEADS)[:, None] == lane_head[None, :]).astype(F32).reshape(N_HEADS, 1, BW)
    tri = (jnp.arange(chunk)[:, None] >= jnp.arange(chunk)[None, :]).astype(F32)
    return e, tri, masks, hm


def _rwkv(rw, params, consts, bsz, seq, chunk):
    nc = seq // chunk
    t = bsz * seq
    n = N_HEADS * chunk
    mu, w0, w2p, a0, a2p, g2, k_k, k_a, r_k, lg, lb = params
    e, tri, masks, hm = consts
    c2 = lambda shape: pl.BlockSpec(shape, lambda b, c: (0, 0))
    c3 = lambda shape: pl.BlockSpec(shape, lambda b, c: (0, 0, 0))
    return pl.pallas_call(
        functools.partial(_rwkv_kernel, chunk=chunk),
        grid=(bsz, nc),
        in_specs=[pl.BlockSpec((chunk, RWKV_PROJ), lambda b, c: (b * nc + c, 0)),
                  c2((1, RWKV_PROJ)), c2((1, BW)), c2((LANE, BW)), c2((1, BW)), c2((LANE, BW)),
                  c2((LANE, BW)), c2((1, BW)), c2((1, BW)), c2((1, BW)), c2((1, BW)), c2((1, BW)),
                  c2((BW, BW)), c2((chunk, chunk)), c3((6, n, n)), c3((N_HEADS, 1, BW))],
        out_specs=pl.BlockSpec((chunk, BW), lambda b, c: (b * nc + c, 0)),
        out_shape=jax.ShapeDtypeStruct((t, BW), F32),
        scratch_shapes=[pltpu.VMEM((BW, BW), F32), pltpu.VMEM((8, RWKV_PROJ), F32)],
        compiler_params=_cp(("parallel", "arbitrary")),
        name="rwkv7_mix",
    )(rw, mu, w0, w2p, a0, a2p, g2, k_k, k_a, r_k, lg, lb, e, tri, masks, hm)


def _merge_kernel(h_ref, y0_ref, y1_ref, y2_ref, y3_ref, gate_ref, wb_ref, wo_ref, g_ref, b_ref, o_ref):
    merged = None
    for n, y_ref in enumerate((y0_ref, y1_ref, y2_ref, y3_ref)):
        up = jnp.dot(y_ref[...].astype(BF16), wb_ref[n], preferred_element_type=F32)
        term = gate_ref[:, n * D_MODEL:(n + 1) * D_MODEL].astype(F32) * up
        merged = term if merged is None else merged + term
    z = ALPHA * h_ref[...] + jnp.dot(merged.astype(BF16), wo_ref[...], preferred_element_type=F32)
    o_ref[...] = _layer_norm(z, g_ref[...], b_ref[...])


def _merge(h, y_s5, y_fox, y_rwkv, y_sb, gates, wb, wo, g, b, bsz, seq, tm):
    ns = seq // tm
    t = bsz * seq
    row = lambda bi, i: (bi * ns + i, 0)
    return pl.pallas_call(
        _merge_kernel,
        grid=(bsz, ns),
        in_specs=[pl.BlockSpec((tm, D_MODEL), row),
                  pl.BlockSpec((tm, BW), lambda bi, i: (i, bi)),
                  pl.BlockSpec((tm, BW), row), pl.BlockSpec((tm, BW), row), pl.BlockSpec((tm, BW), row),
                  pl.BlockSpec((tm, N_BRANCHES * D_MODEL), row),
                  pl.BlockSpec((N_BRANCHES, BW, D_MODEL), lambda bi, i: (0, 0, 0)),
                  pl.BlockSpec((D_MODEL, D_MODEL), lambda bi, i: (0, 0)),
                  pl.BlockSpec((1, D_MODEL), lambda bi, i: (0, 0)),
                  pl.BlockSpec((1, D_MODEL), lambda bi, i: (0, 0))],
        out_specs=pl.BlockSpec((tm, D_MODEL), row),
        out_shape=jax.ShapeDtypeStruct((t, D_MODEL), F32),
        compiler_params=_cp(("parallel", "parallel")),
        name="merge_ln1",
    )(h, y_s5, y_fox, y_rwkv, y_sb, gates, wb, wo, g, b)


def _mlp_kernel(h_ref, p_ref, w1_ref, w2_ref, pw_ref, pg_ref, g_ref, b_ref, o_ref, *, ff_chunk):
    h = h_ref[...]
    hb = h.astype(BF16)
    ffn = None
    for c in range(D_FF // ff_chunk):
        sl = slice(c * ff_chunk, (c + 1) * ff_chunk)
        hid = jnp.maximum(jnp.dot(hb, w1_ref[:, sl], preferred_element_type=F32), 0.0)
        part = jnp.dot((hid * hid).astype(BF16), w2_ref[sl, :], preferred_element_type=F32)
        ffn = part if ffn is None else ffn + part
    ple = (jax.nn.sigmoid(jnp.dot(hb, pg_ref[...], preferred_element_type=F32))
           * jnp.dot(p_ref[...].astype(BF16), pw_ref[...], preferred_element_type=F32))
    o_ref[...] = _layer_norm(ALPHA * h + ffn + ple, g_ref[...], b_ref[...])


def _mlp(h, p, w1, w2, pw, pg, g, b, tm):
    t = h.shape[0]
    full = lambda shape: pl.BlockSpec(shape, lambda i: (0, 0))
    return pl.pallas_call(
        functools.partial(_mlp_kernel, ff_chunk=1024),
        grid=(t // tm,),
        in_specs=[pl.BlockSpec((tm, D_MODEL), lambda i: (i, 0)),
                  pl.BlockSpec((tm, PLE_DIM), lambda i: (i, 0)),
                  full((D_MODEL, D_FF)), full((D_FF, D_MODEL)), full((PLE_DIM, D_MODEL)),
                  full((D_MODEL, D_MODEL)), full((1, D_MODEL)), full((1, D_MODEL))],
        out_specs=pl.BlockSpec((tm, D_MODEL), lambda i: (i, 0)),
        out_shape=jax.ShapeDtypeStruct((t, D_MODEL), F32),
        compiler_params=_cp(("parallel",)),
        name="mlp_ple_ln2",
    )(h, p, w1, w2, pw, pg, g, b)


def _pad_rows(m, rows, at):
    out = jnp.zeros((rows, m.shape[1]), m.dtype)
    return out.at[at:at + m.shape[0]].set(m)


def kernel(x, p, w_in, s5_lambda_re, s5_lambda_im, s5_log_dt, s5_b_re, s5_b_im, s5_c_re, s5_c_im, s5_d, s5_glu_w, s5_glu_b, fox_f_bias, rwkv_mu, rwkv_w0, rwkv_w2, rwkv_a0, rwkv_a2, rwkv_g2, rwkv_k_k, rwkv_k_a, rwkv_r_k, rwkv_lnx_g, rwkv_lnx_b, w_branch, w_out, ln1_g, ln1_b, mlp_w1, mlp_w2, ple_w, ple_gate_w, ln2_g, ln2_b):
    bsz, seq, _ = x.shape
    t = bsz * seq
    tm = min(512, seq)
    ts = min(128, seq)
    chunk = RWKV_CHUNK
    scale = HEAD_DIM ** -0.5

    rwkv_consts = _rwkv_consts(chunk)
    tri_lane = (jnp.arange(LANE)[:, None] <= jnp.arange(LANE)[None, :]).astype(F32)
    later_key = (jnp.arange(ATT_TILE)[:, None] < jnp.arange(ATT_TILE)[None, :]).astype(BF16)
    row2 = lambda a: a.reshape(1, -1)

    h = x.reshape(t, D_MODEL)
    for i in range(DEPTH):
        w = w_in[i]
        w_main = jnp.concatenate([
            w[:, 0:256], w[:, 256:512] * scale, w[:, 512:768],
            w[:, 1028:2052],
            w[:, 2052:2308] * scale, w[:, 2308:2564]], axis=1).astype(BF16)
        w_vt = jnp.concatenate([w[:, 768:1024].T, w[:, 2564:2820].T], axis=0).astype(BF16)
        w_ff = _pad_rows(w[:, 1024:1028].T, 8, 0).astype(BF16)
        w_gates = w[:, 2820:].astype(BF16)

        u_sb, fqk, rw, sqk, fvt, svt, fft = _proj(h, w_main, w_vt, w_ff, bsz, seq, tm)
        gates = _gates(h, w_gates, tm)

        bbar, cmat, lam = _s5_params(s5_lambda_re[i], s5_lambda_im[i], s5_log_dt[i], s5_b_re[i],
                                     s5_b_im[i], s5_c_re[i], s5_c_im[i], bsz)
        y_s5 = _s5(u_sb.reshape(seq * bsz, BW), bbar, cmat, lam, row2(s5_d[i]),
                   s5_glu_w[i].astype(BF16), row2(s5_glu_b[i]), bsz, seq, ts)
        y_s5 = y_s5.reshape(seq, bsz * BW)

        bias8 = _pad_rows(fox_f_bias[i].reshape(N_HEADS, 1), 8, 0)
        c_row = _fox_c(fft, bias8, tri_lane)
        ccol = jnp.transpose(c_row[:, :N_HEADS], (0, 2, 1)).reshape(t, N_HEADS)
        y_fox = _fox_attn(fqk, fvt, ccol, c_row, bsz, seq)

        rparams = (row2(rwkv_mu[i]), row2(rwkv_w0[i]), _pad_rows(rwkv_w2[i], LANE, 0),
                   row2(rwkv_a0[i]), _pad_rows(rwkv_a2[i], LANE, 64), rwkv_g2[i],
                   row2(rwkv_k_k[i]), row2(rwkv_k_a[i]), row2(rwkv_r_k[i]),
                   row2(rwkv_lnx_g[i]), row2(rwkv_lnx_b[i]))
        y_rwkv = _rwkv(rw, rparams, rwkv_consts, bsz, seq, chunk)

        y_sb = _sb_attn(sqk, svt, later_key, bsz, seq)

        h = _merge(h, y_s5, y_fox, y_rwkv, y_sb, gates, w_branch[i].astype(BF16),
                   w_out[i].astype(BF16), row2(ln1_g[i]), row2(ln1_b[i]), bsz, seq, tm)
        h = _mlp(h, p[i].reshape(t, PLE_DIM), mlp_w1[i].astype(BF16), mlp_w2[i].astype(BF16),
                 ple_w[i].astype(BF16), ple_gate_w[i].astype(BF16), row2(ln2_g[i]), row2(ln2_b[i]), tm)
    return h.reshape(bsz, seq, D_MODEL)
```

```python
import functools
import math

import jax
import jax.numpy as jnp
from jax import lax
from jax.experimental import pallas as pl
from jax.experimental.pallas import tpu as pltpu

F32 = jnp.float32
BF16 = jnp.bfloat16
HI = lax.Precision.HIGHEST

D_MODEL = 1024
PLE_DIM = 256
N_BRANCHES = 4
BW = 256
HEAD_DIM = 64
N_HEADS = 4
S5_GROUP_CH = 16
S5_GROUPS = 16
S5_STATE = 64
S5_N = S5_GROUPS * S5_STATE
RWKV_PROJ = 1024
D_FF = 4096
LN_EPS = 1e-5
GN_EPS = 64e-5
DEPTH = 2
ALPHA = (2 * DEPTH) ** 0.25
LANE = 128
RWKV_CHUNK = 64
ATT_TILE = 256
VMEM_LIMIT = 56 * 1024 * 1024


def _cp(sem, vmem=VMEM_LIMIT):
    return pltpu.CompilerParams(dimension_semantics=sem, vmem_limit_bytes=vmem)


def _dot(a, b):
    return jnp.dot(a.astype(BF16), b.astype(BF16), preferred_element_type=F32)


def _dot_hi(a, b):
    return jnp.dot(a, b, precision=HI, preferred_element_type=F32)


def _dot_nt(a, b):
    return lax.dot_general(a.astype(BF16), b.astype(BF16), (((1,), (1,)), ((), ())),
                           preferred_element_type=F32)


def _dot_tn(a, b):
    return lax.dot_general(a.astype(BF16), b.astype(BF16), (((0,), (0,)), ((), ())),
                           preferred_element_type=F32)


def _softplus(x):
    return jnp.maximum(x, 0.0) + jnp.log1p(jnp.exp(-jnp.abs(x)))


def _gelu_tanh(x):
    return 0.5 * x * (1.0 + jnp.tanh(math.sqrt(2.0 / math.pi) * (x + 0.044715 * (x * x * x))))


def _layer_norm(z, g, b):
    mu = jnp.mean(z, axis=-1, keepdims=True)
    d = z - mu
    var = jnp.mean(d * d, axis=-1, keepdims=True)
    return d * lax.rsqrt(var + LN_EPS) * g + b


def _proj_kernel(h_ref, wm_ref, wvt_ref, wf_ref, u_ref, fqk_ref, rw_ref, sqk_ref, fvt_ref, svt_ref,
                 fft_ref, *, tm):
    hb = h_ref[...].astype(BF16)
    u_ref[...] = jnp.dot(hb, wm_ref[:, 0:256], preferred_element_type=F32)
    fqk_ref[...] = jnp.dot(hb, wm_ref[:, 256:768], preferred_element_type=F32).astype(BF16)
    rw_ref[...] = jnp.dot(hb, wm_ref[:, 768:1792], preferred_element_type=F32)
    sqk_ref[...] = jnp.dot(hb, wm_ref[:, 1792:2304], preferred_element_type=F32).astype(BF16)
    vt = _dot_nt(wvt_ref[...], hb).astype(BF16)
    for jj in range(tm // ATT_TILE):
        sl = slice(jj * ATT_TILE, (jj + 1) * ATT_TILE)
        fvt_ref[0, jj] = vt[0:BW, sl]
        svt_ref[0, jj] = vt[BW:2 * BW, sl]
    fft_ref[0] = _dot_nt(wf_ref[...], hb)


def _proj(h, wm, wvt, wf, bsz, seq, tm):
    ns = seq // tm
    t = bsz * seq
    row = lambda b, i: (b * ns + i, 0)
    vt_spec = pl.BlockSpec((1, tm // ATT_TILE, BW, ATT_TILE), lambda b, i: (b, i, 0, 0))
    vt_shape = jax.ShapeDtypeStruct((bsz, seq // ATT_TILE, BW, ATT_TILE), BF16)
    return pl.pallas_call(
        functools.partial(_proj_kernel, tm=tm),
        grid=(bsz, ns),
        in_specs=[pl.BlockSpec((tm, D_MODEL), row),
                  pl.BlockSpec((D_MODEL, 2304), lambda b, i: (0, 0)),
                  pl.BlockSpec((2 * BW, D_MODEL), lambda b, i: (0, 0)),
                  pl.BlockSpec((8, D_MODEL), lambda b, i: (0, 0))],
        out_specs=[pl.BlockSpec((tm, BW), lambda b, i: (i, b)),
                   pl.BlockSpec((tm, 2 * BW), row),
                   pl.BlockSpec((tm, RWKV_PROJ), row),
                   pl.BlockSpec((tm, 2 * BW), row),
                   vt_spec, vt_spec,
                   pl.BlockSpec((1, 8, tm), lambda b, i: (b, 0, i))],
        out_shape=[jax.ShapeDtypeStruct((seq, bsz * BW), F32),
                   jax.ShapeDtypeStruct((t, 2 * BW), BF16),
                   jax.ShapeDtypeStruct((t, RWKV_PROJ), F32),
                   jax.ShapeDtypeStruct((t, 2 * BW), BF16),
                   vt_shape, vt_shape,
                   jax.ShapeDtypeStruct((bsz, 8, seq), F32)],
        compiler_params=_cp(("parallel", "parallel")),
        name="in_proj",
    )(h, wm, wvt, wf)


def _gates_kernel(h_ref, wg_ref, o_ref):
    hb = h_ref[...].astype(BF16)
    for c in range(N_BRANCHES):
        sl = slice(c * D_MODEL, (c + 1) * D_MODEL)
        o_ref[:, sl] = jax.nn.sigmoid(
            jnp.dot(hb, wg_ref[:, sl], preferred_element_type=F32)).astype(BF16)


def _gates(h, wg, tm):
    t = h.shape[0]
    return pl.pallas_call(
        _gates_kernel,
        grid=(t // tm,),
        in_specs=[pl.BlockSpec((tm, D_MODEL), lambda i: (i, 0)),
                  pl.BlockSpec((D_MODEL, N_BRANCHES * D_MODEL), lambda i: (0, 0))],
        out_specs=pl.BlockSpec((tm, N_BRANCHES * D_MODEL), lambda i: (i, 0)),
        out_shape=jax.ShapeDtypeStruct((t, N_BRANCHES * D_MODEL), BF16),
        compiler_params=_cp(("parallel",)),
        name="gate_proj",
    )(h, wg)


def _s5_kernel(u_ref, bbar_ref, cmat_ref, lam_ref, d_ref, gw_ref, gb_ref, y_ref, xs_scr, st_scr,
               *, ts, bsz):
    @pl.when(pl.program_id(0) == 0)
    def _():
        st_scr[...] = jnp.zeros(st_scr.shape, F32)

    u = u_ref[...]
    xs_scr[...] = jnp.dot(u.astype(BF16), bbar_ref[...], preferred_element_type=F32)
    lr = lam_ref[:, 0:S5_N]
    li = lam_ref[:, S5_N:2 * S5_N]

    def body(t, carry):
        xr, xi = carry
        r0 = pl.multiple_of(t * bsz, bsz)
        br = xs_scr[pl.ds(r0, bsz), 0:S5_N]
        bi = xs_scr[pl.ds(r0, bsz), S5_N:2 * S5_N]
        nr = lr * xr - li * xi + br
        ni = lr * xi + li * xr + bi
        xs_scr[pl.ds(r0, bsz), 0:S5_N] = nr
        xs_scr[pl.ds(r0, bsz), S5_N:2 * S5_N] = ni
        return nr, ni

    xr, xi = lax.fori_loop(0, ts, body, (st_scr[:, 0:S5_N], st_scr[:, S5_N:2 * S5_N]), unroll=4)
    st_scr[:, 0:S5_N] = xr
    st_scr[:, S5_N:2 * S5_N] = xi

    y = jnp.dot(xs_scr[...].astype(BF16), cmat_ref[...], preferred_element_type=F32) + d_ref[...] * u
    y = _gelu_tanh(y)
    y_ref[...] = y * jax.nn.sigmoid(_dot(y, gw_ref[...]) + gb_ref[...])


def _s5(u_sb, bbar, cmat, lam, dvec, gw, gb, bsz, seq, ts):
    rows = ts * bsz
    full = lambda shape: pl.BlockSpec(shape, lambda i: (0, 0))
    return pl.pallas_call(
        functools.partial(_s5_kernel, ts=ts, bsz=bsz),
        grid=(seq // ts,),
        in_specs=[pl.BlockSpec((rows, BW), lambda i: (i, 0)),
                  full((BW, 2 * S5_N)), full((2 * S5_N, BW)), full((bsz, 2 * S5_N)),
                  full((1, BW)), full((BW, BW)), full((1, BW))],
        out_specs=pl.BlockSpec((rows, BW), lambda i: (i, 0)),
        out_shape=jax.ShapeDtypeStruct((seq * bsz, BW), F32),
        scratch_shapes=[pltpu.VMEM((rows, 2 * S5_N), F32), pltpu.VMEM((bsz, 2 * S5_N), F32)],
        compiler_params=_cp(("arbitrary",)),
        name="s5_mixer",
    )(u_sb, bbar, cmat, lam, dvec, gw, gb)


def _s5_params(lam_re, lam_im, log_dt, b_re, b_im, c_re, c_im, bsz):
    dt = jnp.exp(log_dt)[:, None]
    mag = jnp.exp(lam_re * dt)
    ang = lam_im * dt
    lb_re = mag * jnp.cos(ang)
    lb_im = mag * jnp.sin(ang)
    den = jnp.square(lam_re) + jnp.square(lam_im)
    nr = lb_re - 1.0
    f_re = (nr * lam_re + lb_im * lam_im) / den
    f_im = (lb_im * lam_re - nr * lam_im) / den
    bb_re = f_re[..., None] * b_re - f_im[..., None] * b_im
    bb_im = f_re[..., None] * b_im + f_im[..., None] * b_re
    eye = jnp.eye(S5_GROUPS, dtype=F32)
    bd_in = lambda m: jnp.einsum('gph,gk->ghkp', m, eye).reshape(BW, S5_N)
    bd_out = lambda m: jnp.einsum('ghp,gk->gpkh', m, eye).reshape(S5_N, BW)
    bbar = jnp.concatenate([bd_in(bb_re), bd_in(bb_im)], axis=1).astype(BF16)
    cmat = jnp.concatenate([bd_out(c_re), -bd_out(c_im)], axis=0).astype(BF16)
    lam = jnp.concatenate([lb_re.reshape(1, S5_N), lb_im.reshape(1, S5_N)], axis=1)
    return bbar, cmat, jnp.broadcast_to(lam, (bsz, 2 * S5_N))


def _foxc_kernel(ff_ref, bias_ref, tri_ref, c_ref, *, n_chunks):
    carry = jnp.zeros((8, 1), F32)
    for ch in range(n_chunks):
        sl = slice(ch * LANE, (ch + 1) * LANE)
        x = ff_ref[0, :, sl] + bias_ref[...]
        lf = jnp.minimum(x, 0.0) - jnp.log1p(jnp.exp(-jnp.abs(x)))
        cs = _dot_hi(lf, tri_ref[...]) + carry
        c_ref[0, :, sl] = cs
        carry = cs[:, LANE - 1:LANE]


def _fox_c(fft, bias8, tri):
    bsz, _, seq = fft.shape
    return pl.pallas_call(
        functools.partial(_foxc_kernel, n_chunks=seq // LANE),
        grid=(bsz,),
        in_specs=[pl.BlockSpec((1, 8, seq), lambda b: (b, 0, 0)),
                  pl.BlockSpec((8, 1), lambda b: (0, 0)),
                  pl.BlockSpec((LANE, LANE), lambda b: (0, 0))],
        out_specs=pl.BlockSpec((1, 8, seq), lambda b: (b, 0, 0)),
        out_shape=jax.ShapeDtypeStruct((bsz, 8, seq), F32),
        compiler_params=_cp(("parallel",)),
        name="fox_cumlogf",
    )(fft, bias8, tri)


def _head_select(x, hh):
    lane = lax.broadcasted_iota(jnp.int32, x.shape, 1)
    sel = (lane < HEAD_DIM) if hh == 0 else (lane >= HEAD_DIM)
    return jnp.where(sel, x, jnp.zeros_like(x))


def _pair_cols(h):
    return slice((h // 2) * LANE, (h // 2 + 1) * LANE)


def _head_rows(h):
    return slice(h * HEAD_DIM, (h + 1) * HEAD_DIM)


def _split_heads(q):
    return [_head_select(q[:, _pair_cols(h)], h % 2) for h in range(N_HEADS)]


def _tile_positions():
    kpos = lax.broadcasted_iota(jnp.int32, (ATT_TILE, ATT_TILE), 0)
    qpos = lax.broadcasted_iota(jnp.int32, (ATT_TILE, ATT_TILE), 1)
    return kpos, qpos


def _scores(k_ref, j, qh):
    ks = k_ref[pl.ds(pl.multiple_of(j * ATT_TILE, ATT_TILE), ATT_TILE), :]
    return tuple(_dot_nt(ks[:, _pair_cols(h)], qh[h]) for h in range(N_HEADS))


def _fox_kernel(q_ref, k_ref, vt_ref, ck_ref, cq_ref, o_ref, acc_scr, st_scr, s_scr):
    i = pl.program_id(1)
    qh = _split_heads(q_ref[...])
    cq = [cq_ref[0, h:h + 1, :] for h in range(N_HEADS)]
    hs = range(N_HEADS)
    acc_scr[...] = jnp.zeros(acc_scr.shape, F32)
    for h in hs:
        st_scr[2 * h:2 * h + 1, :] = jnp.full((1, ATT_TILE), -1e30, F32)
        st_scr[2 * h + 1:2 * h + 2, :] = jnp.zeros((1, ATT_TILE), F32)

    def tile(j, s, masked):
        k0 = pl.multiple_of(j * ATT_TILE, ATT_TILE)
        vt = vt_ref[0, j]
        m_old = [st_scr[2 * h:2 * h + 1, :] for h in hs]
        l_old = [st_scr[2 * h + 1:2 * h + 2, :] for h in hs]
        t1 = [s[h] - ck_ref[pl.ds(k0, ATT_TILE), h:h + 1] for h in hs]
        if masked:
            kpos, qpos = _tile_positions()
            t1 = [jnp.where(kpos <= qpos, t, -1e30) for t in t1]
        m_new = [jnp.maximum(m_old[h], jnp.max(t1[h], axis=0, keepdims=True) + cq[h]) for h in hs]
        alpha = [jnp.exp(m_old[h] - m_new[h]) for h in hs]
        p = [jnp.exp(t1[h] - (m_new[h] - cq[h])) for h in hs]
        l = [alpha[h] * l_old[h] + jnp.sum(p[h], axis=0, keepdims=True) for h in hs]
        pv = [jnp.dot(vt[_head_rows(h), :], p[h].astype(BF16), preferred_element_type=F32) for h in hs]
        for h in hs:
            acc_scr[h] = alpha[h] * acc_scr[h] + pv[h]
            st_scr[2 * h:2 * h + 1, :] = m_new[h]
            st_scr[2 * h + 1:2 * h + 2, :] = l[h]

    def held():
        return tuple(s_scr[h] for h in hs)

    def hold(s):
        for h in hs:
            s_scr[h] = s[h]

    def pair(jj, _):
        first = 2 * jj
        s_first = held()
        s_second = _scores(k_ref, first + 1, qh)
        tile(first, s_first, False)
        hold(_scores(k_ref, first + 2, qh))
        tile(first + 1, s_second, False)
        return 0

    hold(_scores(k_ref, 0, qh))
    lax.fori_loop(0, i // 2, pair, 0)

    @pl.when(lax.rem(i, 2) == 1)
    def _():
        s_prev = held()
        s_diag = _scores(k_ref, i, qh)
        tile(i - 1, s_prev, False)
        hold(s_diag)

    tile(i, held(), True)
    out_t = jnp.concatenate([acc_scr[h] / st_scr[2 * h + 1:2 * h + 2, :] for h in hs], axis=0)
    o_ref[...] = out_t.T


def _att_specs(seq, nq):
    return [pl.BlockSpec((ATT_TILE, BW), lambda b, i: (b * nq + i, 0)),
            pl.BlockSpec((seq, BW), lambda b, i: (b, 1)),
            pl.BlockSpec((1, seq // ATT_TILE, BW, ATT_TILE), lambda b, i: (b, 0, 0, 0))]


def _fox_attn(qk, vt, ccol, crow, bsz, seq):
    nq = seq // ATT_TILE
    return pl.pallas_call(
        _fox_kernel,
        grid=(bsz, nq),
        in_specs=_att_specs(seq, nq) + [
            pl.BlockSpec((seq, N_HEADS), lambda b, i: (b, 0)),
            pl.BlockSpec((1, 8, ATT_TILE), lambda b, i: (b, 0, i))],
        out_specs=pl.BlockSpec((ATT_TILE, BW), lambda b, i: (b * nq + i, 0)),
        out_shape=jax.ShapeDtypeStruct((bsz * seq, BW), F32),
        scratch_shapes=[pltpu.VMEM((N_HEADS, HEAD_DIM, ATT_TILE), F32),
                        pltpu.VMEM((2 * N_HEADS, ATT_TILE), F32),
                        pltpu.VMEM((N_HEADS, ATT_TILE, ATT_TILE), F32)],
        compiler_params=_cp(("parallel", "parallel")),
        name="fox_attention",
    )(qk, qk, vt, ccol, crow)


def _sb_kernel(q_ref, k_ref, vt_ref, w_ref, o_ref, acc_scr, st_scr, s_scr):
    i = pl.program_id(1)
    qh = _split_heads(q_ref[...])
    wmat = w_ref[...]
    hs = range(N_HEADS)
    acc_scr[...] = jnp.zeros(acc_scr.shape, F32)
    st_scr[...] = jnp.zeros(st_scr.shape, F32)

    def tile(j, z, masked):
        vt = vt_ref[0, j]
        seen = [st_scr[h:h + 1, :] for h in hs]
        sp = [jnp.maximum(t, 0.0) + jnp.log(1.0 + jnp.exp(-jnp.abs(t))) for t in z]
        if masked:
            kpos, qpos = _tile_positions()
            mask = kpos < qpos
            sp = [jnp.where(mask, t, 0.0) for t in sp]
        later = [jnp.dot(wmat, t.astype(BF16), preferred_element_type=F32) for t in sp]
        w = [jnp.exp((z[h] - sp[h]) - later[h] - seen[h]) for h in hs]
        if masked:
            w = [jnp.where(mask, t, 0.0) for t in w]
        pv = [jnp.dot(vt[_head_rows(h), :], w[h].astype(BF16), preferred_element_type=F32) for h in hs]
        for h in hs:
            acc_scr[h] = acc_scr[h] + pv[h]
            st_scr[h:h + 1, :] = seen[h] + jnp.sum(sp[h], axis=0, keepdims=True)

    def held():
        return tuple(s_scr[h] for h in hs)

    def hold(z):
        for h in hs:
            s_scr[h] = z[h]

    def pair(jj, _):
        first = i - 1 - 2 * jj
        z_first = held()
        z_second = _scores(k_ref, first - 1, qh)
        tile(first, z_first, False)
        hold(_scores(k_ref, jnp.maximum(first - 2, 0), qh))
        tile(first - 1, z_second, False)
        return 0

    z_diag = _scores(k_ref, i, qh)
    hold(_scores(k_ref, jnp.maximum(i - 1, 0), qh))
    tile(i, z_diag, True)
    lax.fori_loop(0, i // 2, pair, 0)

    @pl.when(lax.rem(i, 2) == 1)
    def _():
        tile(0, held(), False)

    o_ref[...] = jnp.concatenate([acc_scr[h] for h in hs], axis=0).T


def _sb_attn(qk, vt, wmat, bsz, seq):
    nq = seq // ATT_TILE
    return pl.pallas_call(
        _sb_kernel,
        grid=(bsz, nq),
        in_specs=_att_specs(seq, nq) + [pl.BlockSpec((ATT_TILE, ATT_TILE), lambda b, i: (0, 0))],
        out_specs=pl.BlockSpec((ATT_TILE, BW), lambda b, i: (b * nq + i, 0)),
        out_shape=jax.ShapeDtypeStruct((bsz * seq, BW), F32),
        scratch_shapes=[pltpu.VMEM((N_HEADS, HEAD_DIM, ATT_TILE), F32),
                        pltpu.VMEM((8, ATT_TILE), F32),
                        pltpu.VMEM((N_HEADS, ATT_TILE, ATT_TILE), F32)],
        compiler_params=_cp(("parallel", "parallel")),
        name="sb_attention",
    )(qk, qk, vt, wmat)


def _split2(x):
    hi = x.astype(BF16)
    return hi, (x - hi.astype(F32)).astype(BF16)


def _dot_w3(x, w_ref):
    hi, lo = _split2(x)
    return (jnp.dot(hi, w_ref[0], preferred_element_type=F32)
            + jnp.dot(lo, w_ref[0], preferred_element_type=F32)
            + jnp.dot(hi, w_ref[1], preferred_element_type=F32))


def _dot_sel2(x, sel):
    hi, lo = _split2(x)
    return jnp.dot(hi, sel, preferred_element_type=F32) + jnp.dot(lo, sel, preferred_element_type=F32)


def _dot_sel3_left(sel, x):
    hi = x.astype(BF16)
    r1 = x - hi.astype(F32)
    mid = r1.astype(BF16)
    lo = (r1 - mid.astype(F32)).astype(BF16)
    return (jnp.dot(sel, hi, preferred_element_type=F32) + jnp.dot(sel, mid, preferred_element_type=F32)
            + jnp.dot(sel, lo, preferred_element_type=F32))


def _rwkv_kernel(rw_ref, mu_ref, w0_ref, w2_ref, a0_ref, a2_ref, g2_ref, kk_ref, ka_ref, rk_ref,
                 lg_ref, lb_ref, e_ref, tri_ref, msk_ref, hm_ref, o_ref, ht_scr, last_scr,
                 *, chunk, nb):
    L = chunk

    @pl.when(pl.program_id(1) == 0)
    def _():
        ht_scr[...] = jnp.zeros(ht_scr.shape, F32)
        last_scr[...] = jnp.zeros(last_scr.shape, F32)

    e = e_ref[...]
    e32 = e.astype(F32)
    bs = range(nb)

    def stack(t):
        return jnp.concatenate([t * hm_ref[h] for h in range(N_HEADS)], axis=0).astype(BF16)

    def each(f, *lists):
        return [f(*args) for args in zip(*lists)]

    def shift_mix(n):
        x = rw_ref[n]
        row = lax.broadcasted_iota(jnp.int32, x.shape, 0)
        prev = jnp.where(row == 0, last_scr[n, 0:1, :], pltpu.roll(x, 1, axis=0))
        last_scr[n, 0:1, :] = x[L - 1:L, :]
        return x + (prev - x) * mu_ref[...]

    xs = each(shift_mix, bs)
    r = [t[:, 0:256] for t in xs]
    k = [t[:, 256:512] for t in xs]
    v = [t[:, 512:768] for t in xs]
    wa = [t[:, 768:896] for t in xs]
    g1 = [t[:, 896:1024] for t in xs]
    wpre = each(lambda t: w0_ref[...] + _dot_w3(jnp.tanh(t), w2_ref), wa)
    ld = each(lambda t: -jnp.exp(-_softplus(-t) - 0.5), wpre)
    a = each(lambda t: jax.nn.sigmoid(a0_ref[...] + _dot_w3(t, a2_ref)), wa)
    g = each(lambda t: _dot_w3(jax.nn.sigmoid(t), g2_ref), g1)
    kk = each(lambda t: t * kk_ref[...], k)
    kap = each(lambda t: t / jnp.maximum(jnp.sqrt(_dot_sel2(t * t, e)), 1e-12), kk)
    k2 = each(lambda t, u: t * (1.0 + (u - 1.0) * ka_ref[...]), k, a)
    beta = each(lambda t, u: t * u, kap, a)

    cs = each(lambda t: _dot_sel3_left(tri_ref[...], t), ld)
    gam = each(jnp.exp, cs)
    ginv = each(lambda t: jnp.exp(-t), cs)
    g_last = [t[L - 1:L, :] for t in gam]

    kts = each(lambda t, c, d: stack(t * jnp.exp(c - d)), kap, cs, ld)
    rts = each(lambda t, u: stack(t * u), r, gam)
    khs = each(lambda t, u: stack(t * u), k2, ginv)
    bhs = each(lambda t, u: stack(t * u), beta, ginv)
    vs = each(stack, v)
    m = each(lambda t, u: _dot_nt(t, u) * msk_ref[0], kts, bhs)
    akk = each(lambda t, u: _dot_nt(t, u) * msk_ref[0], kts, khs)
    ark = each(lambda t, u: _dot_nt(t, u) * msk_ref[1], rts, khs)
    arb = each(lambda t, u: _dot_nt(t, u) * msk_ref[1], rts, bhs)

    md = each(lambda t: t * msk_ref[2], m)
    p2 = each(lambda t: _dot(t, t), md)
    p4 = each(lambda t: _dot(t, t), p2)
    p8 = each(lambda t: _dot(t, t), p4)
    xinv = each(lambda t: msk_ref[5] - t, md)
    for pw in (p2, p4, p8):
        xinv = each(lambda t, u: t + _dot(t, u), xinv, pw)
    for lvl in (3, 4):
        off = each(lambda t, u: _dot(t, u * msk_ref[lvl]), xinv, m)
        xinv = each(lambda t, u: t - _dot(u, t), xinv, off)

    ht = [ht_scr[n] for n in bs]
    wmat = each(lambda t, u, c, d: _dot_nt(t, u) + _dot(c, d), kts, ht, akk, vs)
    us = each(_dot, xinv, wmat)
    ys = each(lambda t, u, c, d, f, w: _dot_nt(t, u) + _dot(c, d) - _dot(f, w), rts, ht, ark, vs, arb, us)
    y = [t[0:L] + t[L:2 * L] + t[2 * L:3 * L] + t[3 * L:4 * L] for t in ys]

    lhs = each(lambda t, u: jnp.concatenate([t, u.astype(BF16)], axis=0), vs, us)
    rhs = each(lambda t, u, c, d: jnp.concatenate([stack(t * c * d), stack(-(u * c * d))], axis=0),
               k2, beta, ginv, g_last)
    hnew = each(lambda t, u, c, d: t * u + _dot_tn(c, d) * e32, ht, g_last, lhs, rhs)
    for n in bs:
        ht_scr[n] = hnew[n]

    d = each(lambda t: t - _dot_sel2(t, e) * (1.0 / HEAD_DIM), y)
    yv = each(lambda t: _dot_sel2(t * t, e) * (1.0 / HEAD_DIM), d)
    yn = each(lambda t, u: t * lax.rsqrt(u + GN_EPS) * lg_ref[...] + lb_ref[...], d, yv)
    bonus = each(lambda t, u, c: _dot_sel2(t * u * rk_ref[...], e) * c, r, k2, v)
    for n in bs:
        o_ref[n] = (yn[n] + bonus[n]) * g[n]


def _rwkv_consts(chunk):
    n = N_HEADS * chunk
    idx = jnp.arange(n)
    rh, rt = idx // chunk, idx % chunk
    same = rh[:, None] == rh[None, :]
    blk = lambda s: same & ((rt[:, None] // s) == (rt[None, :] // s))
    masks = jnp.stack([
        same & (rt[:, None] > rt[None, :]),
        same & (rt[:, None] >= rt[None, :]),
        blk(16),
        blk(32) & ~blk(16),
        blk(64) & ~blk(32),
        jnp.eye(n, dtype=bool),
    ]).astype(F32)
    lane_head = jnp.arange(BW) // HEAD_DIM
    e = (lane_head[:, None] == lane_head[None, :]).astype(BF16)
    hm = (jnp.arange(N_HEADS)[:, None] == lane_head[None, :]).astype(F32).reshape(N_HEADS, 1, BW)
    tri = (jnp.arange(chunk)[:, None] >= jnp.arange(chunk)[None, :]).astype(BF16)
    return e, tri, masks, hm


def _hi_lo(w):
    hi = w.astype(BF16)
    return jnp.stack([hi, (w - hi.astype(F32)).astype(BF16)])


def _rwkv(rw, params, consts, bsz, seq, chunk, nb):
    nc = seq // chunk
    n = N_HEADS * chunk
    mu, w0, w2p, a0, a2p, g2, k_k, k_a, r_k, lg, lb = params
    e, tri, masks, hm = consts
    c2 = lambda shape: pl.BlockSpec(shape, lambda b, c: (0, 0))
    c3 = lambda shape: pl.BlockSpec(shape, lambda b, c: (0, 0, 0))
    out = pl.pallas_call(
        functools.partial(_rwkv_kernel, chunk=chunk, nb=nb),
        grid=(bsz // nb, nc),
        in_specs=[pl.BlockSpec((nb, chunk, RWKV_PROJ), lambda b, c: (b, c, 0)),
                  c2((1, RWKV_PROJ)), c2((1, BW)), c3((2, LANE, BW)), c2((1, BW)), c3((2, LANE, BW)),
                  c3((2, LANE, BW)), c2((1, BW)), c2((1, BW)), c2((1, BW)), c2((1, BW)), c2((1, BW)),
                  c2((BW, BW)), c2((chunk, chunk)), c3((6, n, n)), c3((N_HEADS, 1, BW))],
        out_specs=pl.BlockSpec((nb, chunk, BW), lambda b, c: (b, c, 0)),
        out_shape=jax.ShapeDtypeStruct((bsz, seq, BW), F32),
        scratch_shapes=[pltpu.VMEM((nb, BW, BW), F32), pltpu.VMEM((nb, 8, RWKV_PROJ), F32)],
        compiler_params=_cp(("parallel", "arbitrary")),
        name="rwkv7_mix",
    )(rw.reshape(bsz, seq, RWKV_PROJ), mu, w0, _hi_lo(w2p), a0, _hi_lo(a2p), _hi_lo(g2),
      k_k, k_a, r_k, lg, lb, e, tri, masks, hm)
    return out.reshape(bsz * seq, BW)


def _merge_kernel(h_ref, y0_ref, y1_ref, y2_ref, y3_ref, gate_ref, wb_ref, wo_ref, g_ref, b_ref, o_ref):
    merged = None
    for n, y_ref in enumerate((y0_ref, y1_ref, y2_ref, y3_ref)):
        up = jnp.dot(y_ref[...].astype(BF16), wb_ref[n], preferred_element_type=F32)
        term = gate_ref[:, n * D_MODEL:(n + 1) * D_MODEL].astype(F32) * up
        merged = term if merged is None else merged + term
    z = ALPHA * h_ref[...] + jnp.dot(merged.astype(BF16), wo_ref[...], preferred_element_type=F32)
    o_ref[...] = _layer_norm(z, g_ref[...], b_ref[...])


def _merge(h, y_s5, y_fox, y_rwkv, y_sb, gates, wb, wo, g, b, bsz, seq, tm):
    ns = seq // tm
    t = bsz * seq
    row = lambda bi, i: (bi * ns + i, 0)
    return pl.pallas_call(
        _merge_kernel,
        grid=(bsz, ns),
        in_specs=[pl.BlockSpec((tm, D_MODEL), row),
                  pl.BlockSpec((tm, BW), lambda bi, i: (i, bi)),
                  pl.BlockSpec((tm, BW), row), pl.BlockSpec((tm, BW), row), pl.BlockSpec((tm, BW), row),
                  pl.BlockSpec((tm, N_BRANCHES * D_MODEL), row),
                  pl.BlockSpec((N_BRANCHES, BW, D_MODEL), lambda bi, i: (0, 0, 0)),
                  pl.BlockSpec((D_MODEL, D_MODEL), lambda bi, i: (0, 0)),
                  pl.BlockSpec((1, D_MODEL), lambda bi, i: (0, 0)),
                  pl.BlockSpec((1, D_MODEL), lambda bi, i: (0, 0))],
        out_specs=pl.BlockSpec((tm, D_MODEL), row),
        out_shape=jax.ShapeDtypeStruct((t, D_MODEL), F32),
        compiler_params=_cp(("parallel", "parallel")),
        name="merge_ln1",
    )(h, y_s5, y_fox, y_rwkv, y_sb, gates, wb, wo, g, b)


def _mlp_kernel(h_ref, p_ref, w1_ref, w2_ref, pw_ref, pg_ref, g_ref, b_ref, o_ref, *, ff_chunk):
    h = h_ref[...]
    hb = h.astype(BF16)
    ffn = None
    for c in range(D_FF // ff_chunk):
        sl = slice(c * ff_chunk, (c + 1) * ff_chunk)
        hid = jnp.maximum(jnp.dot(hb, w1_ref[:, sl], preferred_element_type=F32), 0.0)
        part = jnp.dot((hid * hid).astype(BF16), w2_ref[sl, :], preferred_element_type=F32)
        ffn = part if ffn is None else ffn + part
    ple = (jax.nn.sigmoid(jnp.dot(hb, pg_ref[...], preferred_element_type=F32))
           * jnp.dot(p_ref[...].astype(BF16), pw_ref[...], preferred_element_type=F32))
    o_ref[...] = _layer_norm(ALPHA * h + ffn + ple, g_ref[...], b_ref[...])


def _mlp(h, p, w1, w2, pw, pg, g, b, tm):
    t = h.shape[0]
    full = lambda shape: pl.BlockSpec(shape, lambda i: (0, 0))
    return pl.pallas_call(
        functools.partial(_mlp_kernel, ff_chunk=1024),
        grid=(t // tm,),
        in_specs=[pl.BlockSpec((tm, D_MODEL), lambda i: (i, 0)),
                  pl.BlockSpec((tm, PLE_DIM), lambda i: (i, 0)),
                  full((D_MODEL, D_FF)), full((D_FF, D_MODEL)), full((PLE_DIM, D_MODEL)),
                  full((D_MODEL, D_MODEL)), full((1, D_MODEL)), full((1, D_MODEL))],
        out_specs=pl.BlockSpec((tm, D_MODEL), lambda i: (i, 0)),
        out_shape=jax.ShapeDtypeStruct((t, D_MODEL), F32),
        compiler_params=_cp(("parallel",)),
        name="mlp_ple_ln2",
    )(h, p, w1, w2, pw, pg, g, b)


def _pad_rows(m, rows, at):
    out = jnp.zeros((rows, m.shape[1]), m.dtype)
    return out.at[at:at + m.shape[0]].set(m)


def kernel(x, p, w_in, s5_lambda_re, s5_lambda_im, s5_log_dt, s5_b_re, s5_b_im, s5_c_re, s5_c_im, s5_d, s5_glu_w, s5_glu_b, fox_f_bias, rwkv_mu, rwkv_w0, rwkv_w2, rwkv_a0, rwkv_a2, rwkv_g2, rwkv_k_k, rwkv_k_a, rwkv_r_k, rwkv_lnx_g, rwkv_lnx_b, w_branch, w_out, ln1_g, ln1_b, mlp_w1, mlp_w2, ple_w, ple_gate_w, ln2_g, ln2_b):
    bsz, seq, _ = x.shape
    t = bsz * seq
    tm = min(512, seq)
    ts = min(128, seq)
    chunk = RWKV_CHUNK
    scale = HEAD_DIM ** -0.5

    rwkv_consts = _rwkv_consts(chunk)
    tri_lane = (jnp.arange(LANE)[:, None] <= jnp.arange(LANE)[None, :]).astype(F32)
    later_key = (jnp.arange(ATT_TILE)[:, None] < jnp.arange(ATT_TILE)[None, :]).astype(BF16)
    row2 = lambda a: a.reshape(1, -1)

    h = x.reshape(t, D_MODEL)
    for i in range(DEPTH):
        w = w_in[i]
        w_main = jnp.concatenate([
            w[:, 0:256], w[:, 256:512] * scale, w[:, 512:768],
            w[:, 1028:2052],
            w[:, 2052:2308] * scale, w[:, 2308:2564]], axis=1).astype(BF16)
        w_vt = jnp.concatenate([w[:, 768:1024].T, w[:, 2564:2820].T], axis=0).astype(BF16)
        w_ff = _pad_rows(w[:, 1024:1028].T, 8, 0).astype(BF16)
        w_gates = w[:, 2820:].astype(BF16)

        u_sb, fqk, rw, sqk, fvt, svt, fft = _proj(h, w_main, w_vt, w_ff, bsz, seq, tm)
        gates = _gates(h, w_gates, tm)

        bbar, cmat, lam = _s5_params(s5_lambda_re[i], s5_lambda_im[i], s5_log_dt[i], s5_b_re[i],
                                     s5_b_im[i], s5_c_re[i], s5_c_im[i], bsz)
        y_s5 = _s5(u_sb.reshape(seq * bsz, BW), bbar, cmat, lam, row2(s5_d[i]),
                   s5_glu_w[i].astype(BF16), row2(s5_glu_b[i]), bsz, seq, ts)
        y_s5 = y_s5.reshape(seq, bsz * BW)

        bias8 = _pad_rows(fox_f_bias[i].reshape(N_HEADS, 1), 8, 0)
        c_row = _fox_c(fft, bias8, tri_lane)
        ccol = jnp.transpose(c_row[:, :N_HEADS], (0, 2, 1)).reshape(t, N_HEADS)
        y_fox = _fox_attn(fqk, fvt, ccol, c_row, bsz, seq)

        rparams = (row2(rwkv_mu[i]), row2(rwkv_w0[i]), _pad_rows(rwkv_w2[i], LANE, 0),
                   row2(rwkv_a0[i]), _pad_rows(rwkv_a2[i], LANE, 64), rwkv_g2[i],
                   row2(rwkv_k_k[i]), row2(rwkv_k_a[i]), row2(rwkv_r_k[i]),
                   row2(rwkv_lnx_g[i]), row2(rwkv_lnx_b[i]))
        y_rwkv = _rwkv(rw, rparams, rwkv_consts, bsz, seq, chunk, 4 if bsz % 4 == 0 else 1)

        y_sb = _sb_attn(sqk, svt, later_key, bsz, seq)

        h = _merge(h, y_s5, y_fox, y_rwkv, y_sb, gates, w_branch[i].astype(BF16),
                   w_out[i].astype(BF16), row2(ln1_g[i]), row2(ln1_b[i]), bsz, seq, tm)
        h = _mlp(h, p[i].reshape(t, PLE_DIM), mlp_w1[i].astype(BF16), mlp_w2[i].astype(BF16),
                 ple_w[i].astype(BF16), ple_gate_w[i].astype(BF16), row2(ln2_g[i]), row2(ln2_b[i]), tm)
    return h.reshape(bsz, seq, D_MODEL)
```

```python
import functools
import math

import jax
import jax.numpy as jnp
from jax import lax
from jax.experimental import pallas as pl
from jax.experimental.pallas import tpu as pltpu

F32 = jnp.float32
BF16 = jnp.bfloat16
HI = lax.Precision.HIGHEST

D_MODEL = 1024
PLE_DIM = 256
N_BRANCHES = 4
BW = 256
HEAD_DIM = 64
N_HEADS = 4
S5_GROUP_CH = 16
S5_GROUPS = 16
S5_STATE = 64
S5_N = S5_GROUPS * S5_STATE
RWKV_PROJ = 1024
D_FF = 4096
LN_EPS = 1e-5
GN_EPS = 64e-5
DEPTH = 2
ALPHA = (2 * DEPTH) ** 0.25
LOG2E = math.log2(math.e)
LANE = 128
RWKV_CHUNK = 64
ATT_TILE = 256
VMEM_LIMIT = 56 * 1024 * 1024


def _cp(sem, vmem=VMEM_LIMIT):
    return pltpu.CompilerParams(dimension_semantics=sem, vmem_limit_bytes=vmem)


def _dot(a, b):
    return jnp.dot(a.astype(BF16), b.astype(BF16), preferred_element_type=F32)


def _dot_hi(a, b):
    return jnp.dot(a, b, precision=HI, preferred_element_type=F32)


def _dot_nt(a, b):
    return lax.dot_general(a.astype(BF16), b.astype(BF16), (((1,), (1,)), ((), ())),
                           preferred_element_type=F32)


def _dot_tn(a, b):
    return lax.dot_general(a.astype(BF16), b.astype(BF16), (((0,), (0,)), ((), ())),
                           preferred_element_type=F32)


def _softplus(x):
    return jnp.maximum(x, 0.0) + jnp.log1p(jnp.exp(-jnp.abs(x)))


def _neg_abs(x):
    bits = lax.bitcast_convert_type(x, jnp.uint32) | jnp.uint32(0x80000000)
    return lax.bitcast_convert_type(bits, F32)


def _gelu_tanh(x):
    return 0.5 * x * (1.0 + jnp.tanh(math.sqrt(2.0 / math.pi) * (x + 0.044715 * (x * x * x))))


def _layer_norm(z, g, b):
    mu = jnp.mean(z, axis=-1, keepdims=True)
    d = z - mu
    var = jnp.mean(d * d, axis=-1, keepdims=True)
    return d * lax.rsqrt(var + LN_EPS) * g + b


def _proj_kernel(h_ref, wm_ref, wvt_ref, wf_ref, u_ref, fqk_ref, rw_ref, sqk_ref, fvt_ref, svt_ref,
                 fft_ref, *, tm):
    hb = h_ref[...].astype(BF16)
    u_ref[...] = jnp.dot(hb, wm_ref[:, 0:256], preferred_element_type=F32)
    fqk_ref[...] = jnp.dot(hb, wm_ref[:, 256:768], preferred_element_type=F32).astype(BF16)
    rw_ref[...] = jnp.dot(hb, wm_ref[:, 768:1792], preferred_element_type=F32)
    sqk_ref[...] = jnp.dot(hb, wm_ref[:, 1792:2304], preferred_element_type=F32).astype(BF16)
    vt = _dot_nt(wvt_ref[...], hb).astype(BF16)
    for jj in range(tm // ATT_TILE):
        sl = slice(jj * ATT_TILE, (jj + 1) * ATT_TILE)
        fvt_ref[0, jj] = vt[0:BW, sl]
        svt_ref[0, jj] = vt[BW:2 * BW, sl]
    fft_ref[0] = _dot_nt(wf_ref[...], hb)


def _proj(h, wm, wvt, wf, bsz, seq, tm):
    ns = seq // tm
    t = bsz * seq
    row = lambda b, i: (b * ns + i, 0)
    vt_spec = pl.BlockSpec((1, tm // ATT_TILE, BW, ATT_TILE), lambda b, i: (b, i, 0, 0))
    vt_shape = jax.ShapeDtypeStruct((bsz, seq // ATT_TILE, BW, ATT_TILE), BF16)
    return pl.pallas_call(
        functools.partial(_proj_kernel, tm=tm),
        grid=(bsz, ns),
        in_specs=[pl.BlockSpec((tm, D_MODEL), row),
                  pl.BlockSpec((D_MODEL, 2304), lambda b, i: (0, 0)),
                  pl.BlockSpec((2 * BW, D_MODEL), lambda b, i: (0, 0)),
                  pl.BlockSpec((8, D_MODEL), lambda b, i: (0, 0))],
        out_specs=[pl.BlockSpec((tm, BW), lambda b, i: (i, b)),
                   pl.BlockSpec((tm, 2 * BW), row),
                   pl.BlockSpec((tm, RWKV_PROJ), row),
                   pl.BlockSpec((tm, 2 * BW), row),
                   vt_spec, vt_spec,
                   pl.BlockSpec((1, 8, tm), lambda b, i: (b, 0, i))],
        out_shape=[jax.ShapeDtypeStruct((seq, bsz * BW), F32),
                   jax.ShapeDtypeStruct((t, 2 * BW), BF16),
                   jax.ShapeDtypeStruct((t, RWKV_PROJ), F32),
                   jax.ShapeDtypeStruct((t, 2 * BW), BF16),
                   vt_shape, vt_shape,
                   jax.ShapeDtypeStruct((bsz, 8, seq), F32)],
        compiler_params=_cp(("parallel", "parallel")),
        name="in_proj",
    )(h, wm, wvt, wf)


def _s5_kernel(u_ref, bbar_ref, cmat_ref, lam_ref, d_ref, gw_ref, gb_ref, y_ref, xs_scr, st_scr,
               *, ts, bsz):
    @pl.when(pl.program_id(0) == 0)
    def _():
        st_scr[...] = jnp.zeros(st_scr.shape, F32)

    u = u_ref[...]
    xs_scr[...] = jnp.dot(u.astype(BF16), bbar_ref[...], preferred_element_type=F32)
    lr = lam_ref[:, 0:S5_N]
    li = lam_ref[:, S5_N:2 * S5_N]

    def body(t, carry):
        xr, xi = carry
        r0 = pl.multiple_of(t * bsz, bsz)
        br = xs_scr[pl.ds(r0, bsz), 0:S5_N]
        bi = xs_scr[pl.ds(r0, bsz), S5_N:2 * S5_N]
        nr = lr * xr - li * xi + br
        ni = lr * xi + li * xr + bi
        xs_scr[pl.ds(r0, bsz), 0:S5_N] = nr
        xs_scr[pl.ds(r0, bsz), S5_N:2 * S5_N] = ni
        return nr, ni

    xr, xi = lax.fori_loop(0, ts, body, (st_scr[:, 0:S5_N], st_scr[:, S5_N:2 * S5_N]), unroll=4)
    st_scr[:, 0:S5_N] = xr
    st_scr[:, S5_N:2 * S5_N] = xi

    y = jnp.dot(xs_scr[...].astype(BF16), cmat_ref[...], preferred_element_type=F32) + d_ref[...] * u
    y = _gelu_tanh(y)
    y_ref[...] = y * jax.nn.sigmoid(_dot(y, gw_ref[...]) + gb_ref[...])


def _s5(u_sb, bbar, cmat, lam, dvec, gw, gb, bsz, seq, ts):
    rows = ts * bsz
    full = lambda shape: pl.BlockSpec(shape, lambda i: (0, 0))
    return pl.pallas_call(
        functools.partial(_s5_kernel, ts=ts, bsz=bsz),
        grid=(seq // ts,),
        in_specs=[pl.BlockSpec((rows, BW), lambda i: (i, 0)),
                  full((BW, 2 * S5_N)), full((2 * S5_N, BW)), full((bsz, 2 * S5_N)),
                  full((1, BW)), full((BW, BW)), full((1, BW))],
        out_specs=pl.BlockSpec((rows, BW), lambda i: (i, 0)),
        out_shape=jax.ShapeDtypeStruct((seq * bsz, BW), F32),
        scratch_shapes=[pltpu.VMEM((rows, 2 * S5_N), F32), pltpu.VMEM((bsz, 2 * S5_N), F32)],
        compiler_params=_cp(("arbitrary",)),
        name="s5_mixer",
    )(u_sb, bbar, cmat, lam, dvec, gw, gb)


def _s5_params(lam_re, lam_im, log_dt, b_re, b_im, c_re, c_im, bsz):
    dt = jnp.exp(log_dt)[:, None]
    mag = jnp.exp(lam_re * dt)
    ang = lam_im * dt
    lb_re = mag * jnp.cos(ang)
    lb_im = mag * jnp.sin(ang)
    den = jnp.square(lam_re) + jnp.square(lam_im)
    nr = lb_re - 1.0
    f_re = (nr * lam_re + lb_im * lam_im) / den
    f_im = (lb_im * lam_re - nr * lam_im) / den
    bb_re = f_re[..., None] * b_re - f_im[..., None] * b_im
    bb_im = f_re[..., None] * b_im + f_im[..., None] * b_re
    eye = jnp.eye(S5_GROUPS, dtype=F32)
    bd_in = lambda m: jnp.einsum('gph,gk->ghkp', m, eye).reshape(BW, S5_N)
    bd_out = lambda m: jnp.einsum('ghp,gk->gpkh', m, eye).reshape(S5_N, BW)
    bbar = jnp.concatenate([bd_in(bb_re), bd_in(bb_im)], axis=1).astype(BF16)
    cmat = jnp.concatenate([bd_out(c_re), -bd_out(c_im)], axis=0).astype(BF16)
    lam = jnp.concatenate([lb_re.reshape(1, S5_N), lb_im.reshape(1, S5_N)], axis=1)
    return bbar, cmat, jnp.broadcast_to(lam, (bsz, 2 * S5_N))


def _foxc_kernel(ff_ref, bias_ref, tri_ref, c_ref, *, n_chunks):
    carry = jnp.zeros((8, 1), F32)
    for ch in range(n_chunks):
        sl = slice(ch * LANE, (ch + 1) * LANE)
        x = ff_ref[0, :, sl] + bias_ref[...]
        lf = jnp.minimum(x, 0.0) - jnp.log1p(jnp.exp(-jnp.abs(x)))
        cs = _dot_hi(lf, tri_ref[...]) + carry
        c_ref[0, :, sl] = cs * LOG2E
        carry = cs[:, LANE - 1:LANE]


def _fox_c(fft, bias8, tri):
    bsz, _, seq = fft.shape
    return pl.pallas_call(
        functools.partial(_foxc_kernel, n_chunks=seq // LANE),
        grid=(bsz,),
        in_specs=[pl.BlockSpec((1, 8, seq), lambda b: (b, 0, 0)),
                  pl.BlockSpec((8, 1), lambda b: (0, 0)),
                  pl.BlockSpec((LANE, LANE), lambda b: (0, 0))],
        out_specs=pl.BlockSpec((1, 8, seq), lambda b: (b, 0, 0)),
        out_shape=jax.ShapeDtypeStruct((bsz, 8, seq), F32),
        compiler_params=_cp(("parallel",)),
        name="fox_cumlogf",
    )(fft, bias8, tri)


def _head_select(x, hh):
    lane = lax.broadcasted_iota(jnp.int32, x.shape, 1)
    sel = (lane < HEAD_DIM) if hh == 0 else (lane >= HEAD_DIM)
    return jnp.where(sel, x, jnp.zeros_like(x))


def _pair_cols(h):
    return slice((h // 2) * LANE, (h // 2 + 1) * LANE)


def _head_rows(h):
    return slice(h * HEAD_DIM, (h + 1) * HEAD_DIM)


def _split_heads(q):
    return [_head_select(q[:, _pair_cols(h)], h % 2) for h in range(N_HEADS)]


def _tile_positions():
    kpos = lax.broadcasted_iota(jnp.int32, (ATT_TILE, ATT_TILE), 0)
    qpos = lax.broadcasted_iota(jnp.int32, (ATT_TILE, ATT_TILE), 1)
    return kpos, qpos


def _scores(k_ref, j, qh):
    ks = k_ref[pl.ds(pl.multiple_of(j * ATT_TILE, ATT_TILE), ATT_TILE), :]
    return tuple(_dot_nt(ks[:, _pair_cols(h)], qh[h]) for h in range(N_HEADS))


def _fox_kernel(q_ref, k_ref, vt_ref, ck_ref, cq_ref, o_ref, acc_scr, st_scr, s_scr):
    i = pl.program_id(1)
    qh = _split_heads(q_ref[...])
    cq = [cq_ref[0, h:h + 1, :] for h in range(N_HEADS)]
    hs = range(N_HEADS)
    acc_scr[...] = jnp.zeros(acc_scr.shape, F32)
    for h in hs:
        st_scr[2 * h:2 * h + 1, :] = jnp.full((1, ATT_TILE), -1e30, F32)
        st_scr[2 * h + 1:2 * h + 2, :] = jnp.zeros((1, ATT_TILE), F32)

    def tile(j, s, masked):
        k0 = pl.multiple_of(j * ATT_TILE, ATT_TILE)
        vt = vt_ref[0, j]
        m_old = [st_scr[2 * h:2 * h + 1, :] for h in hs]
        l_old = [st_scr[2 * h + 1:2 * h + 2, :] for h in hs]
        t1 = [s[h] - ck_ref[pl.ds(k0, ATT_TILE), h:h + 1] for h in hs]
        if masked:
            kpos, qpos = _tile_positions()
            t1 = [jnp.where(kpos <= qpos, t, -1e30) for t in t1]
        m_new = [jnp.maximum(m_old[h], jnp.max(t1[h], axis=0, keepdims=True) + cq[h]) for h in hs]
        alpha = [jnp.exp2(m_old[h] - m_new[h]) for h in hs]
        p = [jnp.exp2(t1[h] - (m_new[h] - cq[h])) for h in hs]
        l = [alpha[h] * l_old[h] + jnp.sum(p[h], axis=0, keepdims=True) for h in hs]
        pv = [jnp.dot(vt[_head_rows(h), :], p[h].astype(BF16), preferred_element_type=F32) for h in hs]
        for h in hs:
            acc_scr[h] = alpha[h] * acc_scr[h] + pv[h]
            st_scr[2 * h:2 * h + 1, :] = m_new[h]
            st_scr[2 * h + 1:2 * h + 2, :] = l[h]

    def held():
        return tuple(s_scr[h] for h in hs)

    def hold(s):
        for h in hs:
            s_scr[h] = s[h]

    def pair(jj, _):
        first = 2 * jj
        s_first = held()
        s_second = _scores(k_ref, first + 1, qh)
        tile(first, s_first, False)
        hold(_scores(k_ref, first + 2, qh))
        tile(first + 1, s_second, False)
        return 0

    hold(_scores(k_ref, 0, qh))
    lax.fori_loop(0, i // 2, pair, 0)

    @pl.when(lax.rem(i, 2) == 1)
    def _():
        s_prev = held()
        s_diag = _scores(k_ref, i, qh)
        tile(i - 1, s_prev, False)
        hold(s_diag)

    tile(i, held(), True)
    out_t = jnp.concatenate([acc_scr[h] / st_scr[2 * h + 1:2 * h + 2, :] for h in hs], axis=0)
    o_ref[...] = out_t.T


def _att_specs(seq, nq):
    return [pl.BlockSpec((ATT_TILE, BW), lambda b, i: (b * nq + i, 0)),
            pl.BlockSpec((seq, BW), lambda b, i: (b, 1)),
            pl.BlockSpec((1, seq // ATT_TILE, BW, ATT_TILE), lambda b, i: (b, 0, 0, 0))]


def _fox_attn(qk, vt, ccol, crow, bsz, seq):
    nq = seq // ATT_TILE
    return pl.pallas_call(
        _fox_kernel,
        grid=(bsz, nq),
        in_specs=_att_specs(seq, nq) + [
            pl.BlockSpec((seq, N_HEADS), lambda b, i: (b, 0)),
            pl.BlockSpec((1, 8, ATT_TILE), lambda b, i: (b, 0, i))],
        out_specs=pl.BlockSpec((ATT_TILE, BW), lambda b, i: (b * nq + i, 0)),
        out_shape=jax.ShapeDtypeStruct((bsz * seq, BW), F32),
        scratch_shapes=[pltpu.VMEM((N_HEADS, HEAD_DIM, ATT_TILE), F32),
                        pltpu.VMEM((2 * N_HEADS, ATT_TILE), F32),
                        pltpu.VMEM((N_HEADS, ATT_TILE, ATT_TILE), F32)],
        compiler_params=_cp(("parallel", "parallel")),
        name="fox_attention",
    )(qk, qk, vt, ccol, crow)


def _sb_kernel(q_ref, k_ref, vt_ref, w_ref, o_ref, acc_scr, st_scr, s_scr):
    i = pl.program_id(1)
    qh = _split_heads(q_ref[...])
    wmat = w_ref[...]
    hs = range(N_HEADS)
    acc_scr[...] = jnp.zeros(acc_scr.shape, F32)
    st_scr[...] = jnp.zeros(st_scr.shape, F32)

    def tile(j, z, masked):
        vt = vt_ref[0, j]
        seen = [st_scr[h:h + 1, :] for h in hs]
        sp = [jnp.maximum(t, 0.0) + jnp.log(1.0 + jnp.exp2(_neg_abs(t))) * LOG2E for t in z]
        if masked:
            kpos, qpos = _tile_positions()
            mask = kpos < qpos
            sp = [jnp.where(mask, t, 0.0) for t in sp]
        later = [jnp.dot(wmat, t.astype(BF16), preferred_element_type=F32) for t in sp]
        w = [jnp.exp2((z[h] - sp[h]) - later[h] - seen[h]) for h in hs]
        if masked:
            w = [jnp.where(mask, t, 0.0) for t in w]
        pv = [jnp.dot(vt[_head_rows(h), :], w[h].astype(BF16), preferred_element_type=F32) for h in hs]
        for h in hs:
            acc_scr[h] = acc_scr[h] + pv[h]
            st_scr[h:h + 1, :] = seen[h] + jnp.sum(sp[h], axis=0, keepdims=True)

    def held():
        return tuple(s_scr[h] for h in hs)

    def hold(z):
        for h in hs:
            s_scr[h] = z[h]

    def pair(jj, _):
        first = i - 1 - 2 * jj
        z_first = held()
        z_second = _scores(k_ref, first - 1, qh)
        tile(first, z_first, False)
        hold(_scores(k_ref, jnp.maximum(first - 2, 0), qh))
        tile(first - 1, z_second, False)
        return 0

    z_diag = _scores(k_ref, i, qh)
    hold(_scores(k_ref, jnp.maximum(i - 1, 0), qh))
    tile(i, z_diag, True)
    lax.fori_loop(0, i // 2, pair, 0)

    @pl.when(lax.rem(i, 2) == 1)
    def _():
        tile(0, held(), False)

    o_ref[...] = jnp.concatenate([acc_scr[h] for h in hs], axis=0).T


def _sb_attn(qk, vt, wmat, bsz, seq):
    nq = seq // ATT_TILE
    return pl.pallas_call(
        _sb_kernel,
        grid=(bsz, nq),
        in_specs=_att_specs(seq, nq) + [pl.BlockSpec((ATT_TILE, ATT_TILE), lambda b, i: (0, 0))],
        out_specs=pl.BlockSpec((ATT_TILE, BW), lambda b, i: (b * nq + i, 0)),
        out_shape=jax.ShapeDtypeStruct((bsz * seq, BW), F32),
        scratch_shapes=[pltpu.VMEM((N_HEADS, HEAD_DIM, ATT_TILE), F32),
                        pltpu.VMEM((8, ATT_TILE), F32),
                        pltpu.VMEM((N_HEADS, ATT_TILE, ATT_TILE), F32)],
        compiler_params=_cp(("parallel", "parallel")),
        name="sb_attention",
    )(qk, qk, vt, wmat)


def _split2(x):
    hi = x.astype(BF16)
    return hi, (x - hi.astype(F32)).astype(BF16)


def _dot_w3(x, w_ref):
    hi, lo = _split2(x)
    return (jnp.dot(hi, w_ref[0], preferred_element_type=F32)
            + jnp.dot(lo, w_ref[0], preferred_element_type=F32)
            + jnp.dot(hi, w_ref[1], preferred_element_type=F32))


def _dot_sel2(x, sel):
    hi, lo = _split2(x)
    return jnp.dot(hi, sel, preferred_element_type=F32) + jnp.dot(lo, sel, preferred_element_type=F32)


def _dot_sel3_left(sel, x):
    hi = x.astype(BF16)
    r1 = x - hi.astype(F32)
    mid = r1.astype(BF16)
    lo = (r1 - mid.astype(F32)).astype(BF16)
    return (jnp.dot(sel, hi, preferred_element_type=F32) + jnp.dot(sel, mid, preferred_element_type=F32)
            + jnp.dot(sel, lo, preferred_element_type=F32))


def _rwkv_kernel(rw_ref, mu_ref, w0_ref, w2_ref, a0_ref, a2_ref, g2_ref, kk_ref, ka_ref, rk_ref,
                 lg_ref, lb_ref, e_ref, tri_ref, msk_ref, hm_ref, o_ref, ht_scr, last_scr,
                 *, chunk, nb):
    L = chunk

    @pl.when(pl.program_id(1) == 0)
    def _():
        ht_scr[...] = jnp.zeros(ht_scr.shape, F32)
        last_scr[...] = jnp.zeros(last_scr.shape, F32)

    e = e_ref[...]
    e32 = e.astype(F32)
    bs = range(nb)

    def stack(t):
        return jnp.concatenate([t * hm_ref[h] for h in range(N_HEADS)], axis=0).astype(BF16)

    def each(f, *lists):
        return [f(*args) for args in zip(*lists)]

    def shift_mix(n):
        x = rw_ref[n]
        row = lax.broadcasted_iota(jnp.int32, x.shape, 0)
        prev = jnp.where(row == 0, last_scr[n, 0:1, :], pltpu.roll(x, 1, axis=0))
        last_scr[n, 0:1, :] = x[L - 1:L, :]
        return x + (prev - x) * mu_ref[...]

    xs = each(shift_mix, bs)
    r = [t[:, 0:256] for t in xs]
    k = [t[:, 256:512] for t in xs]
    v = [t[:, 512:768] for t in xs]
    wa = [t[:, 768:896] for t in xs]
    g1 = [t[:, 896:1024] for t in xs]
    wpre = each(lambda t: w0_ref[...] + _dot_w3(jnp.tanh(t), w2_ref), wa)
    ld = each(lambda t: -jnp.exp(-_softplus(-t) - 0.5), wpre)
    a = each(lambda t: jax.nn.sigmoid(a0_ref[...] + _dot_w3(t, a2_ref)), wa)
    g = each(lambda t: _dot_w3(jax.nn.sigmoid(t), g2_ref), g1)
    kk = each(lambda t: t * kk_ref[...], k)
    kap = each(lambda t: t / jnp.maximum(jnp.sqrt(_dot_sel2(t * t, e)), 1e-12), kk)
    k2 = each(lambda t, u: t * (1.0 + (u - 1.0) * ka_ref[...]), k, a)
    beta = each(lambda t, u: t * u, kap, a)

    cs = each(lambda t: _dot_sel3_left(tri_ref[...], t), ld)
    gam = each(jnp.exp, cs)
    ginv = each(lambda t: jnp.exp(-t), cs)
    g_last = [t[L - 1:L, :] for t in gam]

    kts = each(lambda t, c, d: stack(t * jnp.exp(c - d)), kap, cs, ld)
    rts = each(lambda t, u: stack(t * u), r, gam)
    khs = each(lambda t, u: stack(t * u), k2, ginv)
    bhs = each(lambda t, u: stack(t * u), beta, ginv)
    vs = each(stack, v)
    m = each(lambda t, u: _dot_nt(t, u) * msk_ref[0], kts, bhs)
    akk = each(lambda t, u: _dot_nt(t, u) * msk_ref[0], kts, khs)
    ark = each(lambda t, u: _dot_nt(t, u) * msk_ref[1], rts, khs)
    arb = each(lambda t, u: _dot_nt(t, u) * msk_ref[1], rts, bhs)

    md = each(lambda t: t * msk_ref[2], m)
    p2 = each(lambda t: _dot(t, t), md)
    p4 = each(lambda t: _dot(t, t), p2)
    p8 = each(lambda t: _dot(t, t), p4)
    xinv = each(lambda t: msk_ref[5] - t, md)
    for pw in (p2, p4, p8):
        xinv = each(lambda t, u: t + _dot(t, u), xinv, pw)
    for lvl in (3, 4):
        off = each(lambda t, u: _dot(t, u * msk_ref[lvl]), xinv, m)
        xinv = each(lambda t, u: t - _dot(u, t), xinv, off)

    ht = [ht_scr[n] for n in bs]
    wmat = each(lambda t, u, c, d: _dot_nt(t, u) + _dot(c, d), kts, ht, akk, vs)
    us = each(_dot, xinv, wmat)
    ys = each(lambda t, u, c, d, f, w: _dot_nt(t, u) + _dot(c, d) - _dot(f, w), rts, ht, ark, vs, arb, us)
    y = [t[0:L] + t[L:2 * L] + t[2 * L:3 * L] + t[3 * L:4 * L] for t in ys]

    lhs = each(lambda t, u: jnp.concatenate([t, u.astype(BF16)], axis=0), vs, us)
    rhs = each(lambda t, u, c, d: jnp.concatenate([stack(t * c * d), stack(-(u * c * d))], axis=0),
               k2, beta, ginv, g_last)
    hnew = each(lambda t, u, c, d: t * u + _dot_tn(c, d) * e32, ht, g_last, lhs, rhs)
    for n in bs:
        ht_scr[n] = hnew[n]

    d = each(lambda t: t - _dot_sel2(t, e) * (1.0 / HEAD_DIM), y)
    yv = each(lambda t: _dot_sel2(t * t, e) * (1.0 / HEAD_DIM), d)
    yn = each(lambda t, u: t * lax.rsqrt(u + GN_EPS) * lg_ref[...] + lb_ref[...], d, yv)
    bonus = each(lambda t, u, c: _dot_sel2(t * u * rk_ref[...], e) * c, r, k2, v)
    for n in bs:
        o_ref[n] = (yn[n] + bonus[n]) * g[n]


def _rwkv_consts(chunk):
    n = N_HEADS * chunk
    idx = jnp.arange(n)
    rh, rt = idx // chunk, idx % chunk
    same = rh[:, None] == rh[None, :]
    blk = lambda s: same & ((rt[:, None] // s) == (rt[None, :] // s))
    masks = jnp.stack([
        same & (rt[:, None] > rt[None, :]),
        same & (rt[:, None] >= rt[None, :]),
        blk(16),
        blk(32) & ~blk(16),
        blk(64) & ~blk(32),
        jnp.eye(n, dtype=bool),
    ]).astype(F32)
    lane_head = jnp.arange(BW) // HEAD_DIM
    e = (lane_head[:, None] == lane_head[None, :]).astype(BF16)
    hm = (jnp.arange(N_HEADS)[:, None] == lane_head[None, :]).astype(F32).reshape(N_HEADS, 1, BW)
    tri = (jnp.arange(chunk)[:, None] >= jnp.arange(chunk)[None, :]).astype(BF16)
    return e, tri, masks, hm


def _hi_lo(w):
    hi = w.astype(BF16)
    return jnp.stack([hi, (w - hi.astype(F32)).astype(BF16)])


def _rwkv(rw, params, consts, bsz, seq, chunk, nb):
    nc = seq // chunk
    n = N_HEADS * chunk
    mu, w0, w2p, a0, a2p, g2, k_k, k_a, r_k, lg, lb = params
    e, tri, masks, hm = consts
    c2 = lambda shape: pl.BlockSpec(shape, lambda b, c: (0, 0))
    c3 = lambda shape: pl.BlockSpec(shape, lambda b, c: (0, 0, 0))
    out = pl.pallas_call(
        functools.partial(_rwkv_kernel, chunk=chunk, nb=nb),
        grid=(bsz // nb, nc),
        in_specs=[pl.BlockSpec((nb, chunk, RWKV_PROJ), lambda b, c: (b, c, 0)),
                  c2((1, RWKV_PROJ)), c2((1, BW)), c3((2, LANE, BW)), c2((1, BW)), c3((2, LANE, BW)),
                  c3((2, LANE, BW)), c2((1, BW)), c2((1, BW)), c2((1, BW)), c2((1, BW)), c2((1, BW)),
                  c2((BW, BW)), c2((chunk, chunk)), c3((6, n, n)), c3((N_HEADS, 1, BW))],
        out_specs=pl.BlockSpec((nb, chunk, BW), lambda b, c: (b, c, 0)),
        out_shape=jax.ShapeDtypeStruct((bsz, seq, BW), F32),
        scratch_shapes=[pltpu.VMEM((nb, BW, BW), F32), pltpu.VMEM((nb, 8, RWKV_PROJ), F32)],
        compiler_params=_cp(("parallel", "arbitrary")),
        name="rwkv7_mix",
    )(rw.reshape(bsz, seq, RWKV_PROJ), mu, w0, _hi_lo(w2p), a0, _hi_lo(a2p), _hi_lo(g2),
      k_k, k_a, r_k, lg, lb, e, tri, masks, hm)
    return out.reshape(bsz * seq, BW)


def _merge_kernel(h_ref, y0_ref, y1_ref, y2_ref, y3_ref, wg_ref, wb_ref, wo_ref, g_ref, b_ref, o_ref):
    h = h_ref[...]
    hb = h.astype(BF16)
    merged = None
    for n, y_ref in enumerate((y0_ref, y1_ref, y2_ref, y3_ref)):
        gate = jax.nn.sigmoid(jnp.dot(hb, wg_ref[:, n * D_MODEL:(n + 1) * D_MODEL],
                                      preferred_element_type=F32))
        up = jnp.dot(y_ref[...].astype(BF16), wb_ref[n], preferred_element_type=F32)
        merged = gate * up if merged is None else merged + gate * up
    z = ALPHA * h + jnp.dot(merged.astype(BF16), wo_ref[...], preferred_element_type=F32)
    o_ref[...] = _layer_norm(z, g_ref[...], b_ref[...])


def _merge(h, y_s5, y_fox, y_rwkv, y_sb, wg, wb, wo, g, b, bsz, seq, tm):
    ns = seq // tm
    t = bsz * seq
    row = lambda bi, i: (bi * ns + i, 0)
    return pl.pallas_call(
        _merge_kernel,
        grid=(bsz, ns),
        in_specs=[pl.BlockSpec((tm, D_MODEL), row),
                  pl.BlockSpec((tm, BW), lambda bi, i: (i, bi)),
                  pl.BlockSpec((tm, BW), row), pl.BlockSpec((tm, BW), row), pl.BlockSpec((tm, BW), row),
                  pl.BlockSpec((D_MODEL, N_BRANCHES * D_MODEL), lambda bi, i: (0, 0)),
                  pl.BlockSpec((N_BRANCHES, BW, D_MODEL), lambda bi, i: (0, 0, 0)),
                  pl.BlockSpec((D_MODEL, D_MODEL), lambda bi, i: (0, 0)),
                  pl.BlockSpec((1, D_MODEL), lambda bi, i: (0, 0)),
                  pl.BlockSpec((1, D_MODEL), lambda bi, i: (0, 0))],
        out_specs=pl.BlockSpec((tm, D_MODEL), row),
        out_shape=jax.ShapeDtypeStruct((t, D_MODEL), F32),
        compiler_params=_cp(("parallel", "parallel")),
        name="merge_ln1",
    )(h, y_s5, y_fox, y_rwkv, y_sb, wg, wb, wo, g, b)


def _mlp_kernel(h_ref, p_ref, w1_ref, w2_ref, pw_ref, pg_ref, g_ref, b_ref, o_ref, *, ff_chunk):
    h = h_ref[...]
    hb = h.astype(BF16)
    ffn = None
    for c in range(D_FF // ff_chunk):
        sl = slice(c * ff_chunk, (c + 1) * ff_chunk)
        hid = jnp.maximum(jnp.dot(hb, w1_ref[:, sl], preferred_element_type=F32), 0.0)
        part = jnp.dot((hid * hid).astype(BF16), w2_ref[sl, :], preferred_element_type=F32)
        ffn = part if ffn is None else ffn + part
    ple = (jax.nn.sigmoid(jnp.dot(hb, pg_ref[...], preferred_element_type=F32))
           * jnp.dot(p_ref[...].astype(BF16), pw_ref[...], preferred_element_type=F32))
    o_ref[...] = _layer_norm(ALPHA * h + ffn + ple, g_ref[...], b_ref[...])


def _mlp(h, p, w1, w2, pw, pg, g, b, tm):
    t = h.shape[0]
    full = lambda shape: pl.BlockSpec(shape, lambda i: (0, 0))
    return pl.pallas_call(
        functools.partial(_mlp_kernel, ff_chunk=1024),
        grid=(t // tm,),
        in_specs=[pl.BlockSpec((tm, D_MODEL), lambda i: (i, 0)),
                  pl.BlockSpec((tm, PLE_DIM), lambda i: (i, 0)),
                  full((D_MODEL, D_FF)), full((D_FF, D_MODEL)), full((PLE_DIM, D_MODEL)),
                  full((D_MODEL, D_MODEL)), full((1, D_MODEL)), full((1, D_MODEL))],
        out_specs=pl.BlockSpec((tm, D_MODEL), lambda i: (i, 0)),
        out_shape=jax.ShapeDtypeStruct((t, D_MODEL), F32),
        compiler_params=_cp(("parallel",)),
        name="mlp_ple_ln2",
    )(h, p, w1, w2, pw, pg, g, b)


def _pad_rows(m, rows, at):
    out = jnp.zeros((rows, m.shape[1]), m.dtype)
    return out.at[at:at + m.shape[0]].set(m)


def kernel(x, p, w_in, s5_lambda_re, s5_lambda_im, s5_log_dt, s5_b_re, s5_b_im, s5_c_re, s5_c_im, s5_d, s5_glu_w, s5_glu_b, fox_f_bias, rwkv_mu, rwkv_w0, rwkv_w2, rwkv_a0, rwkv_a2, rwkv_g2, rwkv_k_k, rwkv_k_a, rwkv_r_k, rwkv_lnx_g, rwkv_lnx_b, w_branch, w_out, ln1_g, ln1_b, mlp_w1, mlp_w2, ple_w, ple_gate_w, ln2_g, ln2_b):
    bsz, seq, _ = x.shape
    t = bsz * seq
    tm = min(512, seq)
    ts = min(128, seq)
    chunk = RWKV_CHUNK
    scale = HEAD_DIM ** -0.5 * LOG2E

    rwkv_consts = _rwkv_consts(chunk)
    tri_lane = (jnp.arange(LANE)[:, None] <= jnp.arange(LANE)[None, :]).astype(F32)
    later_key = (jnp.arange(ATT_TILE)[:, None] < jnp.arange(ATT_TILE)[None, :]).astype(BF16)
    row2 = lambda a: a.reshape(1, -1)

    h = x.reshape(t, D_MODEL)
    for i in range(DEPTH):
        w = w_in[i]
        w_main = jnp.concatenate([
            w[:, 0:256], w[:, 256:512] * scale, w[:, 512:768],
            w[:, 1028:2052],
            w[:, 2052:2308] * scale, w[:, 2308:2564]], axis=1).astype(BF16)
        w_vt = jnp.concatenate([w[:, 768:1024].T, w[:, 2564:2820].T], axis=0).astype(BF16)
        w_ff = _pad_rows(w[:, 1024:1028].T, 8, 0).astype(BF16)
        w_gates = w[:, 2820:].astype(BF16)

        u_sb, fqk, rw, sqk, fvt, svt, fft = _proj(h, w_main, w_vt, w_ff, bsz, seq, tm)

        bbar, cmat, lam = _s5_params(s5_lambda_re[i], s5_lambda_im[i], s5_log_dt[i], s5_b_re[i],
                                     s5_b_im[i], s5_c_re[i], s5_c_im[i], bsz)
        y_s5 = _s5(u_sb.reshape(seq * bsz, BW), bbar, cmat, lam, row2(s5_d[i]),
                   s5_glu_w[i].astype(BF16), row2(s5_glu_b[i]), bsz, seq, ts)
        y_s5 = y_s5.reshape(seq, bsz * BW)

        bias8 = _pad_rows(fox_f_bias[i].reshape(N_HEADS, 1), 8, 0)
        c_row = _fox_c(fft, bias8, tri_lane)
        ccol = jnp.transpose(c_row[:, :N_HEADS], (0, 2, 1)).reshape(t, N_HEADS)
        y_fox = _fox_attn(fqk, fvt, ccol, c_row, bsz, seq)

        rparams = (row2(rwkv_mu[i]), row2(rwkv_w0[i]), _pad_rows(rwkv_w2[i], LANE, 0),
                   row2(rwkv_a0[i]), _pad_rows(rwkv_a2[i], LANE, 64), rwkv_g2[i],
                   row2(rwkv_k_k[i]), row2(rwkv_k_a[i]), row2(rwkv_r_k[i]),
                   row2(rwkv_lnx_g[i]), row2(rwkv_lnx_b[i]))
        y_rwkv = _rwkv(rw, rparams, rwkv_consts, bsz, seq, chunk, 4 if bsz % 4 == 0 else 1)

        y_sb = _sb_attn(sqk, svt, later_key, bsz, seq)

        h = _merge(h, y_s5, y_fox, y_rwkv, y_sb, w_gates, w_branch[i].astype(BF16),
                   w_out[i].astype(BF16), row2(ln1_g[i]), row2(ln1_b[i]), bsz, seq, tm)
        h = _mlp(h, p[i].reshape(t, PLE_DIM), mlp_w1[i].astype(BF16), mlp_w2[i].astype(BF16),
                 ple_w[i].astype(BF16), ple_gate_w[i].astype(BF16), row2(ln2_g[i]), row2(ln2_b[i]), tm)
    return h.reshape(bsz, seq, D_MODEL)
```

```python
import functools
import math

import jax
import jax.numpy as jnp
from jax import lax
from jax.experimental import pallas as pl
from jax.experimental.pallas import tpu as pltpu

F32 = jnp.float32
BF16 = jnp.bfloat16
HI = lax.Precision.HIGHEST

D_MODEL = 1024
PLE_DIM = 256
N_BRANCHES = 4
BW = 256
HEAD_DIM = 64
N_HEADS = 4
S5_GROUP_CH = 16
S5_GROUPS = 16
S5_STATE = 64
S5_N = S5_GROUPS * S5_STATE
RWKV_PROJ = 1024
D_FF = 4096
LN_EPS = 1e-5
GN_EPS = 64e-5
DEPTH = 2
ALPHA = (2 * DEPTH) ** 0.25
LOG2E = math.log2(math.e)
LANE = 128
RWKV_CHUNK = 64
ATT_TILE = 256
SB_DEAD = 160.0
VMEM_LIMIT = 56 * 1024 * 1024


def _cp(sem, vmem=VMEM_LIMIT):
    return pltpu.CompilerParams(dimension_semantics=sem, vmem_limit_bytes=vmem)


def _dot(a, b):
    return jnp.dot(a.astype(BF16), b.astype(BF16), preferred_element_type=F32)


def _dot_hi(a, b):
    return jnp.dot(a, b, precision=HI, preferred_element_type=F32)


def _dot_nt(a, b):
    return lax.dot_general(a.astype(BF16), b.astype(BF16), (((1,), (1,)), ((), ())),
                           preferred_element_type=F32)


def _dot_tn(a, b):
    return lax.dot_general(a.astype(BF16), b.astype(BF16), (((0,), (0,)), ((), ())),
                           preferred_element_type=F32)


def _softplus(x):
    return jnp.maximum(x, 0.0) + jnp.log1p(jnp.exp(-jnp.abs(x)))


def _gelu_tanh(x):
    return 0.5 * x * (1.0 + jnp.tanh(math.sqrt(2.0 / math.pi) * (x + 0.044715 * (x * x * x))))


def _layer_norm(z, g, b):
    mu = jnp.mean(z, axis=-1, keepdims=True)
    d = z - mu
    var = jnp.mean(d * d, axis=-1, keepdims=True)
    return d * lax.rsqrt(var + LN_EPS) * g + b


def _proj_kernel(h_ref, wm_ref, wvt_ref, wf_ref, u_ref, fqk_ref, rw_ref, sqk_ref, fvt_ref, svt_ref,
                 fft_ref, *, tm):
    hb = h_ref[...].astype(BF16)
    u_ref[...] = jnp.dot(hb, wm_ref[:, 0:256], preferred_element_type=F32)
    fqk_ref[...] = jnp.dot(hb, wm_ref[:, 256:768], preferred_element_type=F32).astype(BF16)
    rw_ref[...] = jnp.dot(hb, wm_ref[:, 768:1792], preferred_element_type=F32)
    sqk_ref[...] = jnp.dot(hb, wm_ref[:, 1792:2304], preferred_element_type=F32).astype(BF16)
    vt = _dot_nt(wvt_ref[...], hb).astype(BF16)
    for jj in range(tm // ATT_TILE):
        sl = slice(jj * ATT_TILE, (jj + 1) * ATT_TILE)
        fvt_ref[0, jj] = vt[0:BW, sl]
        svt_ref[0, jj] = vt[BW:2 * BW, sl]
    fft_ref[0] = _dot_nt(wf_ref[...], hb)


def _proj(h, wm, wvt, wf, bsz, seq, tm):
    ns = seq // tm
    t = bsz * seq
    row = lambda b, i: (b * ns + i, 0)
    vt_spec = pl.BlockSpec((1, tm // ATT_TILE, BW, ATT_TILE), lambda b, i: (b, i, 0, 0))
    vt_shape = jax.ShapeDtypeStruct((bsz, seq // ATT_TILE, BW, ATT_TILE), BF16)
    return pl.pallas_call(
        functools.partial(_proj_kernel, tm=tm),
        grid=(bsz, ns),
        in_specs=[pl.BlockSpec((tm, D_MODEL), row),
                  pl.BlockSpec((D_MODEL, 2304), lambda b, i: (0, 0)),
                  pl.BlockSpec((2 * BW, D_MODEL), lambda b, i: (0, 0)),
                  pl.BlockSpec((8, D_MODEL), lambda b, i: (0, 0))],
        out_specs=[pl.BlockSpec((tm, BW), lambda b, i: (i, b)),
                   pl.BlockSpec((tm, 2 * BW), row),
                   pl.BlockSpec((tm, RWKV_PROJ), row),
                   pl.BlockSpec((tm, 2 * BW), row),
                   vt_spec, vt_spec,
                   pl.BlockSpec((1, 8, tm), lambda b, i: (b, 0, i))],
        out_shape=[jax.ShapeDtypeStruct((seq, bsz * BW), F32),
                   jax.ShapeDtypeStruct((t, 2 * BW), BF16),
                   jax.ShapeDtypeStruct((t, RWKV_PROJ), F32),
                   jax.ShapeDtypeStruct((t, 2 * BW), BF16),
                   vt_shape, vt_shape,
                   jax.ShapeDtypeStruct((bsz, 8, seq), F32)],
        compiler_params=_cp(("parallel", "parallel")),
        name="in_proj",
    )(h, wm, wvt, wf)


def _s5_kernel(u_ref, bbar_ref, cmat_ref, lam_ref, d_ref, gw_ref, gb_ref, y_ref, xs_scr, st_scr,
               *, ts, bsz):
    @pl.when(pl.program_id(0) == 0)
    def _():
        st_scr[...] = jnp.zeros(st_scr.shape, F32)

    u = u_ref[...]
    xs_scr[...] = jnp.dot(u.astype(BF16), bbar_ref[...], preferred_element_type=F32)
    lr = lam_ref[:, 0:S5_N]
    li = lam_ref[:, S5_N:2 * S5_N]

    def body(t, carry):
        xr, xi = carry
        r0 = pl.multiple_of(t * bsz, bsz)
        br = xs_scr[pl.ds(r0, bsz), 0:S5_N]
        bi = xs_scr[pl.ds(r0, bsz), S5_N:2 * S5_N]
        nr = lr * xr - li * xi + br
        ni = lr * xi + li * xr + bi
        xs_scr[pl.ds(r0, bsz), 0:S5_N] = nr
        xs_scr[pl.ds(r0, bsz), S5_N:2 * S5_N] = ni
        return nr, ni

    xr, xi = lax.fori_loop(0, ts, body, (st_scr[:, 0:S5_N], st_scr[:, S5_N:2 * S5_N]), unroll=4)
    st_scr[:, 0:S5_N] = xr
    st_scr[:, S5_N:2 * S5_N] = xi

    y = jnp.dot(xs_scr[...].astype(BF16), cmat_ref[...], preferred_element_type=F32) + d_ref[...] * u
    y = _gelu_tanh(y)
    y_ref[...] = y * jax.nn.sigmoid(_dot(y, gw_ref[...]) + gb_ref[...])


def _s5(u_sb, bbar, cmat, lam, dvec, gw, gb, bsz, seq, ts):
    rows = ts * bsz
    full = lambda shape: pl.BlockSpec(shape, lambda i: (0, 0))
    return pl.pallas_call(
        functools.partial(_s5_kernel, ts=ts, bsz=bsz),
        grid=(seq // ts,),
        in_specs=[pl.BlockSpec((rows, BW), lambda i: (i, 0)),
                  full((BW, 2 * S5_N)), full((2 * S5_N, BW)), full((bsz, 2 * S5_N)),
                  full((1, BW)), full((BW, BW)), full((1, BW))],
        out_specs=pl.BlockSpec((rows, BW), lambda i: (i, 0)),
        out_shape=jax.ShapeDtypeStruct((seq * bsz, BW), F32),
        scratch_shapes=[pltpu.VMEM((rows, 2 * S5_N), F32), pltpu.VMEM((bsz, 2 * S5_N), F32)],
        compiler_params=_cp(("arbitrary",)),
        name="s5_mixer",
    )(u_sb, bbar, cmat, lam, dvec, gw, gb)


def _s5_params(lam_re, lam_im, log_dt, b_re, b_im, c_re, c_im, bsz):
    dt = jnp.exp(log_dt)[:, None]
    mag = jnp.exp(lam_re * dt)
    ang = lam_im * dt
    lb_re = mag * jnp.cos(ang)
    lb_im = mag * jnp.sin(ang)
    den = jnp.square(lam_re) + jnp.square(lam_im)
    nr = lb_re - 1.0
    f_re = (nr * lam_re + lb_im * lam_im) / den
    f_im = (lb_im * lam_re - nr * lam_im) / den
    bb_re = f_re[..., None] * b_re - f_im[..., None] * b_im
    bb_im = f_re[..., None] * b_im + f_im[..., None] * b_re
    eye = jnp.eye(S5_GROUPS, dtype=F32)
    bd_in = lambda m: jnp.einsum('gph,gk->ghkp', m, eye).reshape(BW, S5_N)
    bd_out = lambda m: jnp.einsum('ghp,gk->gpkh', m, eye).reshape(S5_N, BW)
    bbar = jnp.concatenate([bd_in(bb_re), bd_in(bb_im)], axis=1).astype(BF16)
    cmat = jnp.concatenate([bd_out(c_re), -bd_out(c_im)], axis=0).astype(BF16)
    lam = jnp.concatenate([lb_re.reshape(1, S5_N), lb_im.reshape(1, S5_N)], axis=1)
    return bbar, cmat, jnp.broadcast_to(lam, (bsz, 2 * S5_N))


def _foxc_kernel(ff_ref, bias_ref, tri_ref, c_ref, *, n_chunks):
    carry = jnp.zeros((8, 1), F32)
    for ch in range(n_chunks):
        sl = slice(ch * LANE, (ch + 1) * LANE)
        x = ff_ref[0, :, sl] + bias_ref[...]
        lf = jnp.minimum(x, 0.0) - jnp.log1p(jnp.exp(-jnp.abs(x)))
        cs = _dot_hi(lf, tri_ref[...]) + carry
        c_ref[0, :, sl] = cs * LOG2E
        carry = cs[:, LANE - 1:LANE]


def _fox_c(fft, bias8, tri):
    bsz, _, seq = fft.shape
    return pl.pallas_call(
        functools.partial(_foxc_kernel, n_chunks=seq // LANE),
        grid=(bsz,),
        in_specs=[pl.BlockSpec((1, 8, seq), lambda b: (b, 0, 0)),
                  pl.BlockSpec((8, 1), lambda b: (0, 0)),
                  pl.BlockSpec((LANE, LANE), lambda b: (0, 0))],
        out_specs=pl.BlockSpec((1, 8, seq), lambda b: (b, 0, 0)),
        out_shape=jax.ShapeDtypeStruct((bsz, 8, seq), F32),
        compiler_params=_cp(("parallel",)),
        name="fox_cumlogf",
    )(fft, bias8, tri)


def _head_select(x, hh):
    lane = lax.broadcasted_iota(jnp.int32, x.shape, 1)
    sel = (lane < HEAD_DIM) if hh == 0 else (lane >= HEAD_DIM)
    return jnp.where(sel, x, jnp.zeros_like(x))


def _pair_cols(h):
    return slice((h // 2) * LANE, (h // 2 + 1) * LANE)


def _head_rows(h):
    return slice(h * HEAD_DIM, (h + 1) * HEAD_DIM)


def _split_heads(q):
    return [_head_select(q[:, _pair_cols(h)], h % 2) for h in range(N_HEADS)]


def _tile_positions():
    kpos = lax.broadcasted_iota(jnp.int32, (ATT_TILE, ATT_TILE), 0)
    qpos = lax.broadcasted_iota(jnp.int32, (ATT_TILE, ATT_TILE), 1)
    return kpos, qpos


def _scores(k_ref, j, qh):
    ks = k_ref[pl.ds(pl.multiple_of(j * ATT_TILE, ATT_TILE), ATT_TILE), :]
    return tuple(_dot_nt(ks[:, _pair_cols(h)], qh[h]) for h in range(N_HEADS))


def _fox_kernel(q_ref, k_ref, vt_ref, ck_ref, cq_ref, o_ref, acc_scr, st_scr, s_scr):
    i = pl.program_id(1)
    qh = _split_heads(q_ref[...])
    cq = [cq_ref[0, h:h + 1, :] for h in range(N_HEADS)]
    hs = range(N_HEADS)
    acc_scr[...] = jnp.zeros(acc_scr.shape, F32)
    for h in hs:
        st_scr[2 * h:2 * h + 1, :] = jnp.full((1, ATT_TILE), -1e30, F32)
        st_scr[2 * h + 1:2 * h + 2, :] = jnp.zeros((1, ATT_TILE), F32)

    def tile(j, s, masked):
        k0 = pl.multiple_of(j * ATT_TILE, ATT_TILE)
        vt = vt_ref[0, j]
        m_old = [st_scr[2 * h:2 * h + 1, :] for h in hs]
        l_old = [st_scr[2 * h + 1:2 * h + 2, :] for h in hs]
        t1 = [s[h] - ck_ref[pl.ds(k0, ATT_TILE), h:h + 1] for h in hs]
        if masked:
            kpos, qpos = _tile_positions()
            t1 = [jnp.where(kpos <= qpos, t, -1e30) for t in t1]
        m_new = [jnp.maximum(m_old[h], jnp.max(t1[h], axis=0, keepdims=True) + cq[h]) for h in hs]
        alpha = [jnp.exp2(m_old[h] - m_new[h]) for h in hs]
        p = [jnp.exp2(t1[h] - (m_new[h] - cq[h])) for h in hs]
        l = [alpha[h] * l_old[h] + jnp.sum(p[h], axis=0, keepdims=True) for h in hs]
        pv = [jnp.dot(vt[_head_rows(h), :], p[h].astype(BF16), preferred_element_type=F32) for h in hs]
        for h in hs:
            acc_scr[h] = alpha[h] * acc_scr[h] + pv[h]
            st_scr[2 * h:2 * h + 1, :] = m_new[h]
            st_scr[2 * h + 1:2 * h + 2, :] = l[h]

    def held():
        return tuple(s_scr[h] for h in hs)

    def hold(s):
        for h in hs:
            s_scr[h] = s[h]

    def pair(jj, _):
        first = 2 * jj
        s_first = held()
        s_second = _scores(k_ref, first + 1, qh)
        tile(first, s_first, False)
        hold(_scores(k_ref, first + 2, qh))
        tile(first + 1, s_second, False)
        return 0

    hold(_scores(k_ref, 0, qh))
    lax.fori_loop(0, i // 2, pair, 0)

    @pl.when(lax.rem(i, 2) == 1)
    def _():
        s_prev = held()
        s_diag = _scores(k_ref, i, qh)
        tile(i - 1, s_prev, False)
        hold(s_diag)

    tile(i, held(), True)
    out_t = jnp.concatenate([acc_scr[h] / st_scr[2 * h + 1:2 * h + 2, :] for h in hs], axis=0)
    o_ref[...] = out_t.T


def _att_specs(seq, nq):
    return [pl.BlockSpec((ATT_TILE, BW), lambda b, i: (b * nq + i, 0)),
            pl.BlockSpec((seq, BW), lambda b, i: (b, 1)),
            pl.BlockSpec((1, seq // ATT_TILE, BW, ATT_TILE), lambda b, i: (b, 0, 0, 0))]


def _fox_attn(qk, vt, ccol, crow, bsz, seq):
    nq = seq // ATT_TILE
    return pl.pallas_call(
        _fox_kernel,
        grid=(bsz, nq),
        in_specs=_att_specs(seq, nq) + [
            pl.BlockSpec((seq, N_HEADS), lambda b, i: (b, 0)),
            pl.BlockSpec((1, 8, ATT_TILE), lambda b, i: (b, 0, i))],
        out_specs=pl.BlockSpec((ATT_TILE, BW), lambda b, i: (b * nq + i, 0)),
        out_shape=jax.ShapeDtypeStruct((bsz * seq, BW), F32),
        scratch_shapes=[pltpu.VMEM((N_HEADS, HEAD_DIM, ATT_TILE), F32),
                        pltpu.VMEM((2 * N_HEADS, ATT_TILE), F32),
                        pltpu.VMEM((N_HEADS, ATT_TILE, ATT_TILE), F32)],
        compiler_params=_cp(("parallel", "parallel")),
        name="fox_attention",
    )(qk, qk, vt, ccol, crow)


def _sb_kernel(q_ref, k_ref, vt_ref, w_ref, o_ref, acc_scr, st_scr, s_scr):
    i = pl.program_id(1)
    qh = _split_heads(q_ref[...])
    wmat = w_ref[...]
    hs = range(N_HEADS)
    acc_scr[...] = jnp.zeros(acc_scr.shape, F32)
    st_scr[...] = jnp.zeros(st_scr.shape, F32)

    def tile(j, z, masked):
        vt = vt_ref[0, j]
        seen = [st_scr[h:h + 1, :] for h in hs]
        sp = [jnp.maximum(t, 0.0) + jnp.log(1.0 + jnp.exp2(-jnp.abs(t))) * LOG2E for t in z]
        if masked:
            kpos, qpos = _tile_positions()
            mask = kpos < qpos
            sp = [jnp.where(mask, t, 0.0) for t in sp]
        later = [jnp.dot(wmat, t.astype(BF16), preferred_element_type=F32) for t in sp]
        w = [jnp.exp2((z[h] - sp[h]) - later[h] - seen[h]) for h in hs]
        if masked:
            w = [jnp.where(mask, t, 0.0) for t in w]
        pv = [jnp.dot(vt[_head_rows(h), :], w[h].astype(BF16), preferred_element_type=F32) for h in hs]
        for h in hs:
            acc_scr[h] = acc_scr[h] + pv[h]
            st_scr[h:h + 1, :] = seen[h] + jnp.sum(sp[h], axis=0, keepdims=True)

    def held():
        return tuple(s_scr[h] for h in hs)

    def hold(z):
        for h in hs:
            s_scr[h] = z[h]

    def all_dead():
        return jnp.min(st_scr[0:N_HEADS, :]) >= SB_DEAD

    def pair(carry):
        jj, _ = carry
        first = i - 1 - 2 * jj
        z_first = held()
        z_second = _scores(k_ref, first - 1, qh)
        tile(first, z_first, False)
        hold(_scores(k_ref, jnp.maximum(first - 2, 0), qh))
        tile(first - 1, z_second, False)
        return jj + 1, all_dead()

    z_diag = _scores(k_ref, i, qh)
    hold(_scores(k_ref, jnp.maximum(i - 1, 0), qh))
    tile(i, z_diag, True)
    _, dead = lax.while_loop(lambda c: jnp.logical_and(c[0] < i // 2, jnp.logical_not(c[1])),
                             pair, (jnp.int32(0), all_dead()))

    @pl.when(jnp.logical_and(lax.rem(i, 2) == 1, jnp.logical_not(dead)))
    def _():
        tile(0, held(), False)

    o_ref[...] = jnp.concatenate([acc_scr[h] for h in hs], axis=0).T


def _sb_attn(qk, vt, wmat, bsz, seq):
    nq = seq // ATT_TILE
    return pl.pallas_call(
        _sb_kernel,
        grid=(bsz, nq),
        in_specs=_att_specs(seq, nq) + [pl.BlockSpec((ATT_TILE, ATT_TILE), lambda b, i: (0, 0))],
        out_specs=pl.BlockSpec((ATT_TILE, BW), lambda b, i: (b * nq + i, 0)),
        out_shape=jax.ShapeDtypeStruct((bsz * seq, BW), F32),
        scratch_shapes=[pltpu.VMEM((N_HEADS, HEAD_DIM, ATT_TILE), F32),
                        pltpu.VMEM((8, ATT_TILE), F32),
                        pltpu.VMEM((N_HEADS, ATT_TILE, ATT_TILE), F32)],
        compiler_params=_cp(("parallel", "parallel")),
        name="sb_attention",
    )(qk, qk, vt, wmat)


def _split2(x):
    hi = x.astype(BF16)
    return hi, (x - hi.astype(F32)).astype(BF16)


def _dot_w3(x, w_ref):
    hi, lo = _split2(x)
    return (jnp.dot(hi, w_ref[0], preferred_element_type=F32)
            + jnp.dot(lo, w_ref[0], preferred_element_type=F32)
            + jnp.dot(hi, w_ref[1], preferred_element_type=F32))


def _dot_sel2(x, sel):
    hi, lo = _split2(x)
    return jnp.dot(hi, sel, preferred_element_type=F32) + jnp.dot(lo, sel, preferred_element_type=F32)


def _dot_sel3_left(sel, x):
    hi = x.astype(BF16)
    r1 = x - hi.astype(F32)
    mid = r1.astype(BF16)
    lo = (r1 - mid.astype(F32)).astype(BF16)
    return (jnp.dot(sel, hi, preferred_element_type=F32) + jnp.dot(sel, mid, preferred_element_type=F32)
            + jnp.dot(sel, lo, preferred_element_type=F32))


def _rwkv_kernel(rw_ref, mu_ref, w0_ref, w2_ref, a0_ref, a2_ref, g2_ref, kk_ref, ka_ref, rk_ref,
                 lg_ref, lb_ref, e_ref, tri_ref, msk_ref, hm_ref, o_ref, ht_scr, last_scr,
                 *, chunk, nb):
    L = chunk

    @pl.when(pl.program_id(1) == 0)
    def _():
        ht_scr[...] = jnp.zeros(ht_scr.shape, F32)
        last_scr[...] = jnp.zeros(last_scr.shape, F32)

    e = e_ref[...]
    e32 = e.astype(F32)
    bs = range(nb)

    def stack(t):
        return jnp.concatenate([t * hm_ref[h] for h in range(N_HEADS)], axis=0).astype(BF16)

    def each(f, *lists):
        return [f(*args) for args in zip(*lists)]

    def shift_mix(n):
        x = rw_ref[n]
        row = lax.broadcasted_iota(jnp.int32, x.shape, 0)
        prev = jnp.where(row == 0, last_scr[n, 0:1, :], pltpu.roll(x, 1, axis=0))
        last_scr[n, 0:1, :] = x[L - 1:L, :]
        return x + (prev - x) * mu_ref[...]

    xs = each(shift_mix, bs)
    r = [t[:, 0:256] for t in xs]
    k = [t[:, 256:512] for t in xs]
    v = [t[:, 512:768] for t in xs]
    wa = [t[:, 768:896] for t in xs]
    g1 = [t[:, 896:1024] for t in xs]
    wpre = each(lambda t: w0_ref[...] + _dot_w3(jnp.tanh(t), w2_ref), wa)
    ld = each(lambda t: -jnp.exp(-_softplus(-t) - 0.5), wpre)
    a = each(lambda t: jax.nn.sigmoid(a0_ref[...] + _dot_w3(t, a2_ref)), wa)
    g = each(lambda t: _dot_w3(jax.nn.sigmoid(t), g2_ref), g1)
    kk = each(lambda t: t * kk_ref[...], k)
    kap = each(lambda t: t / jnp.maximum(jnp.sqrt(_dot_sel2(t * t, e)), 1e-12), kk)
    k2 = each(lambda t, u: t * (1.0 + (u - 1.0) * ka_ref[...]), k, a)
    beta = each(lambda t, u: t * u, kap, a)

    cs = each(lambda t: _dot_sel3_left(tri_ref[...], t), ld)
    gam = each(jnp.exp, cs)
    ginv = each(lambda t: jnp.exp(-t), cs)
    g_last = [t[L - 1:L, :] for t in gam]

    kts = each(lambda t, c, d: stack(t * jnp.exp(c - d)), kap, cs, ld)
    rts = each(lambda t, u: stack(t * u), r, gam)
    khs = each(lambda t, u: stack(t * u), k2, ginv)
    bhs = each(lambda t, u: stack(t * u), beta, ginv)
    vs = each(stack, v)
    m = each(lambda t, u: _dot_nt(t, u) * msk_ref[0], kts, bhs)
    akk = each(lambda t, u: _dot_nt(t, u) * msk_ref[0], kts, khs)
    ark = each(lambda t, u: _dot_nt(t, u) * msk_ref[1], rts, khs)
    arb = each(lambda t, u: _dot_nt(t, u) * msk_ref[1], rts, bhs)

    md = each(lambda t: t * msk_ref[2], m)
    p2 = each(lambda t: _dot(t, t), md)
    p4 = each(lambda t: _dot(t, t), p2)
    p8 = each(lambda t: _dot(t, t), p4)
    xinv = each(lambda t: msk_ref[5] - t, md)
    for pw in (p2, p4, p8):
        xinv = each(lambda t, u: t + _dot(t, u), xinv, pw)
    for lvl in (3, 4):
        off = each(lambda t, u: _dot(t, u * msk_ref[lvl]), xinv, m)
        xinv = each(lambda t, u: t - _dot(u, t), xinv, off)

    ht = [ht_scr[n] for n in bs]
    wmat = each(lambda t, u, c, d: _dot_nt(t, u) + _dot(c, d), kts, ht, akk, vs)
    us = each(_dot, xinv, wmat)
    ys = each(lambda t, u, c, d, f, w: _dot_nt(t, u) + _dot(c, d) - _dot(f, w), rts, ht, ark, vs, arb, us)
    y = [t[0:L] + t[L:2 * L] + t[2 * L:3 * L] + t[3 * L:4 * L] for t in ys]

    lhs = each(lambda t, u: jnp.concatenate([t, u.astype(BF16)], axis=0), vs, us)
    rhs = each(lambda t, u, c, d: jnp.concatenate([stack(t * c * d), stack(-(u * c * d))], axis=0),
               k2, beta, ginv, g_last)
    hnew = each(lambda t, u, c, d: t * u + _dot_tn(c, d) * e32, ht, g_last, lhs, rhs)
    for n in bs:
        ht_scr[n] = hnew[n]

    d = each(lambda t: t - _dot_sel2(t, e) * (1.0 / HEAD_DIM), y)
    yv = each(lambda t: _dot_sel2(t * t, e) * (1.0 / HEAD_DIM), d)
    yn = each(lambda t, u: t * lax.rsqrt(u + GN_EPS) * lg_ref[...] + lb_ref[...], d, yv)
    bonus = each(lambda t, u, c: _dot_sel2(t * u * rk_ref[...], e) * c, r, k2, v)
    for n in bs:
        o_ref[n] = (yn[n] + bonus[n]) * g[n]


def _rwkv_consts(chunk):
    n = N_HEADS * chunk
    idx = jnp.arange(n)
    rh, rt = idx // chunk, idx % chunk
    same = rh[:, None] == rh[None, :]
    blk = lambda s: same & ((rt[:, None] // s) == (rt[None, :] // s))
    masks = jnp.stack([
        same & (rt[:, None] > rt[None, :]),
        same & (rt[:, None] >= rt[None, :]),
        blk(16),
        blk(32) & ~blk(16),
        blk(64) & ~blk(32),
        jnp.eye(n, dtype=bool),
    ]).astype(F32)
    lane_head = jnp.arange(BW) // HEAD_DIM
    e = (lane_head[:, None] == lane_head[None, :]).astype(BF16)
    hm = (jnp.arange(N_HEADS)[:, None] == lane_head[None, :]).astype(F32).reshape(N_HEADS, 1, BW)
    tri = (jnp.arange(chunk)[:, None] >= jnp.arange(chunk)[None, :]).astype(BF16)
    return e, tri, masks, hm


def _hi_lo(w):
    hi = w.astype(BF16)
    return jnp.stack([hi, (w - hi.astype(F32)).astype(BF16)])


def _rwkv(rw, params, consts, bsz, seq, chunk, nb):
    nc = seq // chunk
    n = N_HEADS * chunk
    mu, w0, w2p, a0, a2p, g2, k_k, k_a, r_k, lg, lb = params
    e, tri, masks, hm = consts
    c2 = lambda shape: pl.BlockSpec(shape, lambda b, c: (0, 0))
    c3 = lambda shape: pl.BlockSpec(shape, lambda b, c: (0, 0, 0))
    out = pl.pallas_call(
        functools.partial(_rwkv_kernel, chunk=chunk, nb=nb),
        grid=(bsz // nb, nc),
        in_specs=[pl.BlockSpec((nb, chunk, RWKV_PROJ), lambda b, c: (b, c, 0)),
                  c2((1, RWKV_PROJ)), c2((1, BW)), c3((2, LANE, BW)), c2((1, BW)), c3((2, LANE, BW)),
                  c3((2, LANE, BW)), c2((1, BW)), c2((1, BW)), c2((1, BW)), c2((1, BW)), c2((1, BW)),
                  c2((BW, BW)), c2((chunk, chunk)), c3((6, n, n)), c3((N_HEADS, 1, BW))],
        out_specs=pl.BlockSpec((nb, chunk, BW), lambda b, c: (b, c, 0)),
        out_shape=jax.ShapeDtypeStruct((bsz, seq, BW), F32),
        scratch_shapes=[pltpu.VMEM((nb, BW, BW), F32), pltpu.VMEM((nb, 8, RWKV_PROJ), F32)],
        compiler_params=_cp(("parallel", "arbitrary")),
        name="rwkv7_mix",
    )(rw.reshape(bsz, seq, RWKV_PROJ), mu, w0, _hi_lo(w2p), a0, _hi_lo(a2p), _hi_lo(g2),
      k_k, k_a, r_k, lg, lb, e, tri, masks, hm)
    return out.reshape(bsz * seq, BW)


def _merge_kernel(h_ref, y0_ref, y1_ref, y2_ref, y3_ref, wg_ref, wb_ref, wo_ref, g_ref, b_ref, o_ref):
    h = h_ref[...]
    hb = h.astype(BF16)
    merged = None
    for n, y_ref in enumerate((y0_ref, y1_ref, y2_ref, y3_ref)):
        gate = jax.nn.sigmoid(jnp.dot(hb, wg_ref[:, n * D_MODEL:(n + 1) * D_MODEL],
                                      preferred_element_type=F32))
        up = jnp.dot(y_ref[...].astype(BF16), wb_ref[n], preferred_element_type=F32)
        merged = gate * up if merged is None else merged + gate * up
    z = ALPHA * h + jnp.dot(merged.astype(BF16), wo_ref[...], preferred_element_type=F32)
    o_ref[...] = _layer_norm(z, g_ref[...], b_ref[...])


def _merge(h, y_s5, y_fox, y_rwkv, y_sb, wg, wb, wo, g, b, bsz, seq, tm):
    ns = seq // tm
    t = bsz * seq
    row = lambda bi, i: (bi * ns + i, 0)
    return pl.pallas_call(
        _merge_kernel,
        grid=(bsz, ns),
        in_specs=[pl.BlockSpec((tm, D_MODEL), row),
                  pl.BlockSpec((tm, BW), lambda bi, i: (i, bi)),
                  pl.BlockSpec((tm, BW), row), pl.BlockSpec((tm, BW), row), pl.BlockSpec((tm, BW), row),
                  pl.BlockSpec((D_MODEL, N_BRANCHES * D_MODEL), lambda bi, i: (0, 0)),
                  pl.BlockSpec((N_BRANCHES, BW, D_MODEL), lambda bi, i: (0, 0, 0)),
                  pl.BlockSpec((D_MODEL, D_MODEL), lambda bi, i: (0, 0)),
                  pl.BlockSpec((1, D_MODEL), lambda bi, i: (0, 0)),
                  pl.BlockSpec((1, D_MODEL), lambda bi, i: (0, 0))],
        out_specs=pl.BlockSpec((tm, D_MODEL), row),
        out_shape=jax.ShapeDtypeStruct((t, D_MODEL), F32),
        compiler_params=_cp(("parallel", "parallel")),
        name="merge_ln1",
    )(h, y_s5, y_fox, y_rwkv, y_sb, wg, wb, wo, g, b)


def _mlp_kernel(h_ref, p_ref, w1_ref, w2_ref, pw_ref, pg_ref, g_ref, b_ref, o_ref, *, ff_chunk):
    h = h_ref[...]
    hb = h.astype(BF16)
    ffn = None
    for c in range(D_FF // ff_chunk):
        sl = slice(c * ff_chunk, (c + 1) * ff_chunk)
        hid = jnp.maximum(jnp.dot(hb, w1_ref[:, sl], preferred_element_type=F32), 0.0)
        part = jnp.dot((hid * hid).astype(BF16), w2_ref[sl, :], preferred_element_type=F32)
        ffn = part if ffn is None else ffn + part
    ple = (jax.nn.sigmoid(jnp.dot(hb, pg_ref[...], preferred_element_type=F32))
           * jnp.dot(p_ref[...].astype(BF16), pw_ref[...], preferred_element_type=F32))
    o_ref[...] = _layer_norm(ALPHA * h + ffn + ple, g_ref[...], b_ref[...])


def _mlp(h, p, w1, w2, pw, pg, g, b, tm):
    t = h.shape[0]
    full = lambda shape: pl.BlockSpec(shape, lambda i: (0, 0))
    return pl.pallas_call(
        functools.partial(_mlp_kernel, ff_chunk=1024),
        grid=(t // tm,),
        in_specs=[pl.BlockSpec((tm, D_MODEL), lambda i: (i, 0)),
                  pl.BlockSpec((tm, PLE_DIM), lambda i: (i, 0)),
                  full((D_MODEL, D_FF)), full((D_FF, D_MODEL)), full((PLE_DIM, D_MODEL)),
                  full((D_MODEL, D_MODEL)), full((1, D_MODEL)), full((1, D_MODEL))],
        out_specs=pl.BlockSpec((tm, D_MODEL), lambda i: (i, 0)),
        out_shape=jax.ShapeDtypeStruct((t, D_MODEL), F32),
        compiler_params=_cp(("parallel",)),
        name="mlp_ple_ln2",
    )(h, p, w1, w2, pw, pg, g, b)


def _pad_rows(m, rows, at):
    out = jnp.zeros((rows, m.shape[1]), m.dtype)
    return out.at[at:at + m.shape[0]].set(m)


def kernel(x, p, w_in, s5_lambda_re, s5_lambda_im, s5_log_dt, s5_b_re, s5_b_im, s5_c_re, s5_c_im, s5_d, s5_glu_w, s5_glu_b, fox_f_bias, rwkv_mu, rwkv_w0, rwkv_w2, rwkv_a0, rwkv_a2, rwkv_g2, rwkv_k_k, rwkv_k_a, rwkv_r_k, rwkv_lnx_g, rwkv_lnx_b, w_branch, w_out, ln1_g, ln1_b, mlp_w1, mlp_w2, ple_w, ple_gate_w, ln2_g, ln2_b):
    bsz, seq, _ = x.shape
    t = bsz * seq
    tm = min(512, seq)
    ts = min(128, seq)
    chunk = RWKV_CHUNK
    scale = HEAD_DIM ** -0.5 * LOG2E

    rwkv_consts = _rwkv_consts(chunk)
    tri_lane = (jnp.arange(LANE)[:, None] <= jnp.arange(LANE)[None, :]).astype(F32)
    later_key = (jnp.arange(ATT_TILE)[:, None] < jnp.arange(ATT_TILE)[None, :]).astype(BF16)
    row2 = lambda a: a.reshape(1, -1)

    h = x.reshape(t, D_MODEL)
    for i in range(DEPTH):
        w = w_in[i]
        w_main = jnp.concatenate([
            w[:, 0:256], w[:, 256:512] * scale, w[:, 512:768],
            w[:, 1028:2052],
            w[:, 2052:2308] * scale, w[:, 2308:2564]], axis=1).astype(BF16)
        w_vt = jnp.concatenate([w[:, 768:1024].T, w[:, 2564:2820].T], axis=0).astype(BF16)
        w_ff = _pad_rows(w[:, 1024:1028].T, 8, 0).astype(BF16)
        w_gates = w[:, 2820:].astype(BF16)

        u_sb, fqk, rw, sqk, fvt, svt, fft = _proj(h, w_main, w_vt, w_ff, bsz, seq, tm)

        bbar, cmat, lam = _s5_params(s5_lambda_re[i], s5_lambda_im[i], s5_log_dt[i], s5_b_re[i],
                                     s5_b_im[i], s5_c_re[i], s5_c_im[i], bsz)
        y_s5 = _s5(u_sb.reshape(seq * bsz, BW), bbar, cmat, lam, row2(s5_d[i]),
                   s5_glu_w[i].astype(BF16), row2(s5_glu_b[i]), bsz, seq, ts)
        y_s5 = y_s5.reshape(seq, bsz * BW)

        bias8 = _pad_rows(fox_f_bias[i].reshape(N_HEADS, 1), 8, 0)
        c_row = _fox_c(fft, bias8, tri_lane)
        ccol = jnp.transpose(c_row[:, :N_HEADS], (0, 2, 1)).reshape(t, N_HEADS)
        y_fox = _fox_attn(fqk, fvt, ccol, c_row, bsz, seq)

        rparams = (row2(rwkv_mu[i]), row2(rwkv_w0[i]), _pad_rows(rwkv_w2[i], LANE, 0),
                   row2(rwkv_a0[i]), _pad_rows(rwkv_a2[i], LANE, 64), rwkv_g2[i],
                   row2(rwkv_k_k[i]), row2(rwkv_k_a[i]), row2(rwkv_r_k[i]),
                   row2(rwkv_lnx_g[i]), row2(rwkv_lnx_b[i]))
        y_rwkv = _rwkv(rw, rparams, rwkv_consts, bsz, seq, chunk, 4 if bsz % 4 == 0 else 1)

        y_sb = _sb_attn(sqk, svt, later_key, bsz, seq)

        h = _merge(h, y_s5, y_fox, y_rwkv, y_sb, w_gates, w_branch[i].astype(BF16),
                   w_out[i].astype(BF16), row2(ln1_g[i]), row2(ln1_b[i]), bsz, seq, tm)
        h = _mlp(h, p[i].reshape(t, PLE_DIM), mlp_w1[i].astype(BF16), mlp_w2[i].astype(BF16),
                 ple_w[i].astype(BF16), ple_gate_w[i].astype(BF16), row2(ln2_g[i]), row2(ln2_b[i]), tm)
    return h.reshape(bsz, seq, D_MODEL)
```

```python
import functools
import math

import jax
import jax.numpy as jnp
from jax import lax
from jax.experimental import pallas as pl
from jax.experimental.pallas import tpu as pltpu

F32 = jnp.float32
BF16 = jnp.bfloat16
HI = lax.Precision.HIGHEST

D_MODEL = 1024
PLE_DIM = 256
N_BRANCHES = 4
BW = 256
HEAD_DIM = 64
N_HEADS = 4
S5_GROUP_CH = 16
S5_GROUPS = 16
S5_STATE = 64
S5_N = S5_GROUPS * S5_STATE
RWKV_PROJ = 1024
D_FF = 4096
LN_EPS = 1e-5
GN_EPS = 64e-5
DEPTH = 2
ALPHA = (2 * DEPTH) ** 0.25
LOG2E = math.log2(math.e)
LANE = 128
RWKV_CHUNK = 64
ATT_TILE = 256
SB_DEAD = 160.0
FOX_DEAD = -152.0
VMEM_LIMIT = 56 * 1024 * 1024


def _cp(sem, vmem=VMEM_LIMIT):
    return pltpu.CompilerParams(dimension_semantics=sem, vmem_limit_bytes=vmem)


def _dot(a, b):
    return jnp.dot(a.astype(BF16), b.astype(BF16), preferred_element_type=F32)


def _dot_hi(a, b):
    return jnp.dot(a, b, precision=HI, preferred_element_type=F32)


def _dot_nt(a, b):
    return lax.dot_general(a.astype(BF16), b.astype(BF16), (((1,), (1,)), ((), ())),
                           preferred_element_type=F32)


def _dot_tn(a, b):
    return lax.dot_general(a.astype(BF16), b.astype(BF16), (((0,), (0,)), ((), ())),
                           preferred_element_type=F32)


def _softplus(x):
    return jnp.maximum(x, 0.0) + jnp.log1p(jnp.exp(-jnp.abs(x)))


def _gelu_tanh(x):
    return 0.5 * x * (1.0 + jnp.tanh(math.sqrt(2.0 / math.pi) * (x + 0.044715 * (x * x * x))))


def _layer_norm(z, g, b):
    mu = jnp.mean(z, axis=-1, keepdims=True)
    d = z - mu
    var = jnp.mean(d * d, axis=-1, keepdims=True)
    return d * lax.rsqrt(var + LN_EPS) * g + b


def _proj_kernel(h_ref, wm_ref, wvt_ref, wf_ref, u_ref, fqk_ref, rw_ref, sqk_ref, fvt_ref, svt_ref,
                 fft_ref, *, tm):
    hb = h_ref[...].astype(BF16)
    u_ref[...] = jnp.dot(hb, wm_ref[:, 0:256], preferred_element_type=F32)
    fqk_ref[...] = jnp.dot(hb, wm_ref[:, 256:768], preferred_element_type=F32).astype(BF16)
    rw_ref[...] = jnp.dot(hb, wm_ref[:, 768:1792], preferred_element_type=F32)
    sqk_ref[...] = jnp.dot(hb, wm_ref[:, 1792:2304], preferred_element_type=F32).astype(BF16)
    vt = _dot_nt(wvt_ref[...], hb).astype(BF16)
    for jj in range(tm // ATT_TILE):
        sl = slice(jj * ATT_TILE, (jj + 1) * ATT_TILE)
        fvt_ref[0, jj] = vt[0:BW, sl]
        svt_ref[0, jj] = vt[BW:2 * BW, sl]
    fft_ref[0] = _dot_nt(wf_ref[...], hb)


def _proj(h, wm, wvt, wf, bsz, seq, tm):
    ns = seq // tm
    t = bsz * seq
    row = lambda b, i: (b * ns + i, 0)
    vt_spec = pl.BlockSpec((1, tm // ATT_TILE, BW, ATT_TILE), lambda b, i: (b, i, 0, 0))
    vt_shape = jax.ShapeDtypeStruct((bsz, seq // ATT_TILE, BW, ATT_TILE), BF16)
    return pl.pallas_call(
        functools.partial(_proj_kernel, tm=tm),
        grid=(bsz, ns),
        in_specs=[pl.BlockSpec((tm, D_MODEL), row),
                  pl.BlockSpec((D_MODEL, 2304), lambda b, i: (0, 0)),
                  pl.BlockSpec((2 * BW, D_MODEL), lambda b, i: (0, 0)),
                  pl.BlockSpec((8, D_MODEL), lambda b, i: (0, 0))],
        out_specs=[pl.BlockSpec((tm, BW), lambda b, i: (i, b)),
                   pl.BlockSpec((tm, 2 * BW), row),
                   pl.BlockSpec((tm, RWKV_PROJ), row),
                   pl.BlockSpec((tm, 2 * BW), row),
                   vt_spec, vt_spec,
                   pl.BlockSpec((1, 8, tm), lambda b, i: (b, 0, i))],
        out_shape=[jax.ShapeDtypeStruct((seq, bsz * BW), F32),
                   jax.ShapeDtypeStruct((t, 2 * BW), BF16),
                   jax.ShapeDtypeStruct((t, RWKV_PROJ), F32),
                   jax.ShapeDtypeStruct((t, 2 * BW), BF16),
                   vt_shape, vt_shape,
                   jax.ShapeDtypeStruct((bsz, 8, seq), F32)],
        compiler_params=_cp(("parallel", "parallel")),
        name="in_proj",
    )(h, wm, wvt, wf)


def _s5_kernel(u_ref, bbar_ref, cmat_ref, lam_ref, d_ref, gw_ref, gb_ref, y_ref, xs_scr, st_scr,
               *, ts, bsz):
    @pl.when(pl.program_id(0) == 0)
    def _():
        st_scr[...] = jnp.zeros(st_scr.shape, F32)

    u = u_ref[...]
    xs_scr[...] = jnp.dot(u.astype(BF16), bbar_ref[...], preferred_element_type=F32)
    lr = lam_ref[:, 0:S5_N]
    li = lam_ref[:, S5_N:2 * S5_N]

    def body(t, carry):
        xr, xi = carry
        r0 = pl.multiple_of(t * bsz, bsz)
        br = xs_scr[pl.ds(r0, bsz), 0:S5_N]
        bi = xs_scr[pl.ds(r0, bsz), S5_N:2 * S5_N]
        nr = lr * xr - li * xi + br
        ni = lr * xi + li * xr + bi
        xs_scr[pl.ds(r0, bsz), 0:S5_N] = nr
        xs_scr[pl.ds(r0, bsz), S5_N:2 * S5_N] = ni
        return nr, ni

    xr, xi = lax.fori_loop(0, ts, body, (st_scr[:, 0:S5_N], st_scr[:, S5_N:2 * S5_N]), unroll=4)
    st_scr[:, 0:S5_N] = xr
    st_scr[:, S5_N:2 * S5_N] = xi

    y = jnp.dot(xs_scr[...].astype(BF16), cmat_ref[...], preferred_element_type=F32) + d_ref[...] * u
    y = _gelu_tanh(y)
    y_ref[...] = y * jax.nn.sigmoid(_dot(y, gw_ref[...]) + gb_ref[...])


def _s5(u_sb, bbar, cmat, lam, dvec, gw, gb, bsz, seq, ts):
    rows = ts * bsz
    full = lambda shape: pl.BlockSpec(shape, lambda i: (0, 0))
    return pl.pallas_call(
        functools.partial(_s5_kernel, ts=ts, bsz=bsz),
        grid=(seq // ts,),
        in_specs=[pl.BlockSpec((rows, BW), lambda i: (i, 0)),
                  full((BW, 2 * S5_N)), full((2 * S5_N, BW)), full((bsz, 2 * S5_N)),
                  full((1, BW)), full((BW, BW)), full((1, BW))],
        out_specs=pl.BlockSpec((rows, BW), lambda i: (i, 0)),
        out_shape=jax.ShapeDtypeStruct((seq * bsz, BW), F32),
        scratch_shapes=[pltpu.VMEM((rows, 2 * S5_N), F32), pltpu.VMEM((bsz, 2 * S5_N), F32)],
        compiler_params=_cp(("arbitrary",)),
        name="s5_mixer",
    )(u_sb, bbar, cmat, lam, dvec, gw, gb)


def _s5_params(lam_re, lam_im, log_dt, b_re, b_im, c_re, c_im, bsz):
    dt = jnp.exp(log_dt)[:, None]
    mag = jnp.exp(lam_re * dt)
    ang = lam_im * dt
    lb_re = mag * jnp.cos(ang)
    lb_im = mag * jnp.sin(ang)
    den = jnp.square(lam_re) + jnp.square(lam_im)
    nr = lb_re - 1.0
    f_re = (nr * lam_re + lb_im * lam_im) / den
    f_im = (lb_im * lam_re - nr * lam_im) / den
    bb_re = f_re[..., None] * b_re - f_im[..., None] * b_im
    bb_im = f_re[..., None] * b_im + f_im[..., None] * b_re
    eye = jnp.eye(S5_GROUPS, dtype=F32)
    bd_in = lambda m: jnp.einsum('gph,gk->ghkp', m, eye).reshape(BW, S5_N)
    bd_out = lambda m: jnp.einsum('ghp,gk->gpkh', m, eye).reshape(S5_N, BW)
    bbar = jnp.concatenate([bd_in(bb_re), bd_in(bb_im)], axis=1).astype(BF16)
    cmat = jnp.concatenate([bd_out(c_re), -bd_out(c_im)], axis=0).astype(BF16)
    lam = jnp.concatenate([lb_re.reshape(1, S5_N), lb_im.reshape(1, S5_N)], axis=1)
    return bbar, cmat, jnp.broadcast_to(lam, (bsz, 2 * S5_N))


def _foxc_kernel(ff_ref, bias_ref, tri_ref, c_ref, *, n_chunks):
    carry = jnp.zeros((8, 1), F32)
    for ch in range(n_chunks):
        sl = slice(ch * LANE, (ch + 1) * LANE)
        x = ff_ref[0, :, sl] + bias_ref[...]
        lf = jnp.minimum(x, 0.0) - jnp.log1p(jnp.exp(-jnp.abs(x)))
        cs = _dot_hi(lf, tri_ref[...]) + carry
        c_ref[0, :, sl] = cs * LOG2E
        carry = cs[:, LANE - 1:LANE]


def _fox_c(fft, bias8, tri):
    bsz, _, seq = fft.shape
    return pl.pallas_call(
        functools.partial(_foxc_kernel, n_chunks=seq // LANE),
        grid=(bsz,),
        in_specs=[pl.BlockSpec((1, 8, seq), lambda b: (b, 0, 0)),
                  pl.BlockSpec((8, 1), lambda b: (0, 0)),
                  pl.BlockSpec((LANE, LANE), lambda b: (0, 0))],
        out_specs=pl.BlockSpec((1, 8, seq), lambda b: (b, 0, 0)),
        out_shape=jax.ShapeDtypeStruct((bsz, 8, seq), F32),
        compiler_params=_cp(("parallel",)),
        name="fox_cumlogf",
    )(fft, bias8, tri)


def _head_select(x, hh):
    lane = lax.broadcasted_iota(jnp.int32, x.shape, 1)
    sel = (lane < HEAD_DIM) if hh == 0 else (lane >= HEAD_DIM)
    return jnp.where(sel, x, jnp.zeros_like(x))


def _pair_cols(h):
    return slice((h // 2) * LANE, (h // 2 + 1) * LANE)


def _head_rows(h):
    return slice(h * HEAD_DIM, (h + 1) * HEAD_DIM)


def _split_heads(q):
    return [_head_select(q[:, _pair_cols(h)], h % 2) for h in range(N_HEADS)]


def _tile_positions():
    kpos = lax.broadcasted_iota(jnp.int32, (ATT_TILE, ATT_TILE), 0)
    qpos = lax.broadcasted_iota(jnp.int32, (ATT_TILE, ATT_TILE), 1)
    return kpos, qpos


def _scores(k_ref, j, qh):
    ks = k_ref[pl.ds(pl.multiple_of(j * ATT_TILE, ATT_TILE), ATT_TILE), :]
    return tuple(_dot_nt(ks[:, _pair_cols(h)], qh[h]) for h in range(N_HEADS))


def _head_sq_norms(esel, x):
    xf = x.astype(F32)
    hi, lo = _split2(xf * xf)
    return _dot_nt(esel, hi) + _dot_nt(esel, lo)


def _fox_kernel(q_ref, k_ref, vt_ref, ck_ref, cq_ref, esel_ref, o_ref, acc_scr, st_scr, s_scr, kn_scr,
                *, n_tiles):
    i = pl.program_id(1)
    q = q_ref[...]
    qh = _split_heads(q)
    cq = [cq_ref[0, h:h + 1, :] for h in range(N_HEADS)]
    esel = esel_ref[...]
    hs = range(N_HEADS)
    acc_scr[...] = jnp.zeros(acc_scr.shape, F32)
    for h in hs:
        st_scr[2 * h:2 * h + 1, :] = jnp.full((1, ATT_TILE), -1e30, F32)
        st_scr[2 * h + 1:2 * h + 2, :] = jnp.zeros((1, ATT_TILE), F32)

    @pl.when(i == 0)
    def _():
        def widen(j, mx):
            ks = k_ref[pl.ds(pl.multiple_of(j * ATT_TILE, ATT_TILE), ATT_TILE), :]
            return jnp.maximum(mx, _head_sq_norms(esel, ks))
        mx = lax.fori_loop(0, n_tiles, widen, jnp.zeros((8, ATT_TILE), F32))
        kn_scr[...] = jnp.broadcast_to(jnp.max(mx, axis=1, keepdims=True), kn_scr.shape)

    qn = _head_sq_norms(esel, q)
    reach = [jnp.sqrt(qn[h:h + 1, :] * kn_scr[h:h + 1, 0:1]) * 1.001 + 1.0 + cq[h] for h in hs]

    def tile(j, s, masked):
        k0 = pl.multiple_of(j * ATT_TILE, ATT_TILE)
        vt = vt_ref[0, j]
        m_old = [st_scr[2 * h:2 * h + 1, :] for h in hs]
        l_old = [st_scr[2 * h + 1:2 * h + 2, :] for h in hs]
        t1 = [s[h] - ck_ref[pl.ds(k0, ATT_TILE), h:h + 1] for h in hs]
        if masked:
            kpos, qpos = _tile_positions()
            t1 = [jnp.where(kpos <= qpos, t, -1e30) for t in t1]
        m_new = [jnp.maximum(m_old[h], jnp.max(t1[h], axis=0, keepdims=True) + cq[h]) for h in hs]
        alpha = [jnp.exp2(m_old[h] - m_new[h]) for h in hs]
        p = [jnp.exp2(t1[h] - (m_new[h] - cq[h])) for h in hs]
        l = [alpha[h] * l_old[h] + jnp.sum(p[h], axis=0, keepdims=True) for h in hs]
        pv = [jnp.dot(vt[_head_rows(h), :], p[h].astype(BF16), preferred_element_type=F32) for h in hs]
        for h in hs:
            acc_scr[h] = alpha[h] * acc_scr[h] + pv[h]
            st_scr[2 * h:2 * h + 1, :] = m_new[h]
            st_scr[2 * h + 1:2 * h + 2, :] = l[h]

    def held():
        return tuple(s_scr[h] for h in hs)

    def hold(s):
        for h in hs:
            s_scr[h] = s[h]

    def all_dead(newest_unvisited):
        row = jnp.maximum(newest_unvisited, 0)
        worst = None
        for h in hs:
            gap = reach[h] - ck_ref[pl.ds(row, 1), h:h + 1] - st_scr[2 * h:2 * h + 1, :]
            worst = gap if worst is None else jnp.maximum(worst, gap)
        return jnp.max(worst) < FOX_DEAD

    def pair(carry):
        jj, _ = carry
        first = i - 1 - 2 * jj
        s_first = held()
        s_second = _scores(k_ref, first - 1, qh)
        tile(first, s_first, False)
        hold(_scores(k_ref, jnp.maximum(first - 2, 0), qh))
        tile(first - 1, s_second, False)
        return jj + 1, all_dead((first - 1) * ATT_TILE - 1)

    s_diag = _scores(k_ref, i, qh)
    hold(_scores(k_ref, jnp.maximum(i - 1, 0), qh))
    tile(i, s_diag, True)
    _, dead = lax.while_loop(lambda c: jnp.logical_and(c[0] < i // 2, jnp.logical_not(c[1])),
                             pair, (jnp.int32(0), all_dead(i * ATT_TILE - 1)))

    @pl.when(jnp.logical_and(lax.rem(i, 2) == 1, jnp.logical_not(dead)))
    def _():
        tile(0, held(), False)

    out_t = jnp.concatenate([acc_scr[h] / st_scr[2 * h + 1:2 * h + 2, :] for h in hs], axis=0)
    o_ref[...] = out_t.T


def _att_specs(seq, nq):
    return [pl.BlockSpec((ATT_TILE, BW), lambda b, i: (b * nq + i, 0)),
            pl.BlockSpec((seq, BW), lambda b, i: (b, 1)),
            pl.BlockSpec((1, seq // ATT_TILE, BW, ATT_TILE), lambda b, i: (b, 0, 0, 0))]


def _fox_attn(qk, vt, ccol, crow, esel, bsz, seq):
    nq = seq // ATT_TILE
    return pl.pallas_call(
        functools.partial(_fox_kernel, n_tiles=nq),
        grid=(bsz, nq),
        in_specs=_att_specs(seq, nq) + [
            pl.BlockSpec((seq, N_HEADS), lambda b, i: (b, 0)),
            pl.BlockSpec((1, 8, ATT_TILE), lambda b, i: (b, 0, i)),
            pl.BlockSpec((8, BW), lambda b, i: (0, 0))],
        out_specs=pl.BlockSpec((ATT_TILE, BW), lambda b, i: (b * nq + i, 0)),
        out_shape=jax.ShapeDtypeStruct((bsz * seq, BW), F32),
        scratch_shapes=[pltpu.VMEM((N_HEADS, HEAD_DIM, ATT_TILE), F32),
                        pltpu.VMEM((2 * N_HEADS, ATT_TILE), F32),
                        pltpu.VMEM((N_HEADS, ATT_TILE, ATT_TILE), F32),
                        pltpu.VMEM((8, LANE), F32)],
        compiler_params=_cp(("parallel", "arbitrary")),
        name="fox_attention",
    )(qk, qk, vt, ccol, crow, esel)


def _sb_kernel(q_ref, k_ref, vt_ref, w_ref, o_ref, acc_scr, st_scr, s_scr):
    i = pl.program_id(1)
    qh = _split_heads(q_ref[...])
    wmat = w_ref[...]
    hs = range(N_HEADS)
    acc_scr[...] = jnp.zeros(acc_scr.shape, F32)
    st_scr[...] = jnp.zeros(st_scr.shape, F32)

    def tile(j, z, masked):
        vt = vt_ref[0, j]
        seen = [st_scr[h:h + 1, :] for h in hs]
        sp = [jnp.maximum(t, 0.0) + jnp.log(1.0 + jnp.exp2(-jnp.abs(t))) * LOG2E for t in z]
        if masked:
            kpos, qpos = _tile_positions()
            mask = kpos < qpos
            sp = [jnp.where(mask, t, 0.0) for t in sp]
        later = [jnp.dot(wmat, t.astype(BF16), preferred_element_type=F32) for t in sp]
        w = [jnp.exp2((z[h] - sp[h]) - later[h] - seen[h]) for h in hs]
        if masked:
            w = [jnp.where(mask, t, 0.0) for t in w]
        pv = [jnp.dot(vt[_head_rows(h), :], w[h].astype(BF16), preferred_element_type=F32) for h in hs]
        for h in hs:
            acc_scr[h] = acc_scr[h] + pv[h]
            st_scr[h:h + 1, :] = seen[h] + jnp.sum(sp[h], axis=0, keepdims=True)

    def held():
        return tuple(s_scr[h] for h in hs)

    def hold(z):
        for h in hs:
            s_scr[h] = z[h]

    def all_dead():
        return jnp.min(st_scr[0:N_HEADS, :]) >= SB_DEAD

    def pair(carry):
        jj, _ = carry
        first = i - 2 - 2 * jj
        z_first = held()
        z_second = _scores(k_ref, first - 1, qh)
        tile(first, z_first, False)
        hold(_scores(k_ref, jnp.maximum(first - 2, 0), qh))
        tile(first - 1, z_second, False)
        return jj + 1, all_dead()

    z_diag = _scores(k_ref, i, qh)
    hold(_scores(k_ref, jnp.maximum(i - 1, 0), qh))
    tile(i, z_diag, True)

    @pl.when(i >= 1)
    def _():
        z_prev = held()
        hold(_scores(k_ref, jnp.maximum(i - 2, 0), qh))
        tile(i - 1, z_prev, False)

    rest = jnp.maximum(i - 1, 0)
    _, dead = lax.while_loop(lambda c: jnp.logical_and(c[0] < rest // 2, jnp.logical_not(c[1])),
                             pair, (jnp.int32(0), all_dead()))

    @pl.when(jnp.logical_and(lax.rem(rest, 2) == 1, jnp.logical_not(dead)))
    def _():
        tile(0, held(), False)

    o_ref[...] = jnp.concatenate([acc_scr[h] for h in hs], axis=0).T


def _sb_attn(qk, vt, wmat, bsz, seq):
    nq = seq // ATT_TILE
    return pl.pallas_call(
        _sb_kernel,
        grid=(bsz, nq),
        in_specs=_att_specs(seq, nq) + [pl.BlockSpec((ATT_TILE, ATT_TILE), lambda b, i: (0, 0))],
        out_specs=pl.BlockSpec((ATT_TILE, BW), lambda b, i: (b * nq + i, 0)),
        out_shape=jax.ShapeDtypeStruct((bsz * seq, BW), F32),
        scratch_shapes=[pltpu.VMEM((N_HEADS, HEAD_DIM, ATT_TILE), F32),
                        pltpu.VMEM((8, ATT_TILE), F32),
                        pltpu.VMEM((N_HEADS, ATT_TILE, ATT_TILE), F32)],
        compiler_params=_cp(("parallel", "parallel")),
        name="sb_attention",
    )(qk, qk, vt, wmat)


def _split2(x):
    hi = x.astype(BF16)
    return hi, (x - hi.astype(F32)).astype(BF16)


def _dot_w3(x, w_ref):
    hi, lo = _split2(x)
    return (jnp.dot(hi, w_ref[0], preferred_element_type=F32)
            + jnp.dot(lo, w_ref[0], preferred_element_type=F32)
            + jnp.dot(hi, w_ref[1], preferred_element_type=F32))


def _dot_sel2(x, sel):
    hi, lo = _split2(x)
    return jnp.dot(hi, sel, preferred_element_type=F32) + jnp.dot(lo, sel, preferred_element_type=F32)


def _dot_sel3_left(sel, x):
    hi = x.astype(BF16)
    r1 = x - hi.astype(F32)
    mid = r1.astype(BF16)
    lo = (r1 - mid.astype(F32)).astype(BF16)
    return (jnp.dot(sel, hi, preferred_element_type=F32) + jnp.dot(sel, mid, preferred_element_type=F32)
            + jnp.dot(sel, lo, preferred_element_type=F32))


def _rwkv_kernel(rw_ref, mu_ref, w0_ref, w2_ref, a0_ref, a2_ref, g2_ref, kk_ref, ka_ref, rk_ref,
                 lg_ref, lb_ref, e_ref, tri_ref, msk_ref, hm_ref, o_ref, ht_scr, last_scr,
                 *, chunk, nb):
    L = chunk

    @pl.when(pl.program_id(1) == 0)
    def _():
        ht_scr[...] = jnp.zeros(ht_scr.shape, F32)
        last_scr[...] = jnp.zeros(last_scr.shape, F32)

    e = e_ref[...]
    e32 = e.astype(F32)
    bs = range(nb)

    def stack(t):
        return jnp.concatenate([t * hm_ref[h] for h in range(N_HEADS)], axis=0).astype(BF16)

    def each(f, *lists):
        return [f(*args) for args in zip(*lists)]

    def shift_mix(n):
        x = rw_ref[n]
        row = lax.broadcasted_iota(jnp.int32, x.shape, 0)
        prev = jnp.where(row == 0, last_scr[n, 0:1, :], pltpu.roll(x, 1, axis=0))
        last_scr[n, 0:1, :] = x[L - 1:L, :]
        return x + (prev - x) * mu_ref[...]

    xs = each(shift_mix, bs)
    r = [t[:, 0:256] for t in xs]
    k = [t[:, 256:512] for t in xs]
    v = [t[:, 512:768] for t in xs]
    wa = [t[:, 768:896] for t in xs]
    g1 = [t[:, 896:1024] for t in xs]
    wpre = each(lambda t: w0_ref[...] + _dot_w3(jnp.tanh(t), w2_ref), wa)
    ld = each(lambda t: -jnp.exp(-_softplus(-t) - 0.5), wpre)
    a = each(lambda t: jax.nn.sigmoid(a0_ref[...] + _dot_w3(t, a2_ref)), wa)
    g = each(lambda t: _dot_w3(jax.nn.sigmoid(t), g2_ref), g1)
    kk = each(lambda t: t * kk_ref[...], k)
    kap = each(lambda t: t / jnp.maximum(jnp.sqrt(_dot_sel2(t * t, e)), 1e-12), kk)
    k2 = each(lambda t, u: t * (1.0 + (u - 1.0) * ka_ref[...]), k, a)
    beta = each(lambda t, u: t * u, kap, a)

    cs = each(lambda t: _dot_sel3_left(tri_ref[...], t), ld)
    gam = each(jnp.exp, cs)
    ginv = each(lambda t: jnp.exp(-t), cs)
    g_last = [t[L - 1:L, :] for t in gam]

    kts = each(lambda t, c, d: stack(t * jnp.exp(c - d)), kap, cs, ld)
    rts = each(lambda t, u: stack(t * u), r, gam)
    khs = each(lambda t, u: stack(t * u), k2, ginv)
    bhs = each(lambda t, u: stack(t * u), beta, ginv)
    vs = each(stack, v)
    m = each(lambda t, u: _dot_nt(t, u) * msk_ref[0], kts, bhs)
    akk = each(lambda t, u: _dot_nt(t, u) * msk_ref[0], kts, khs)
    ark = each(lambda t, u: _dot_nt(t, u) * msk_ref[1], rts, khs)
    arb = each(lambda t, u: _dot_nt(t, u) * msk_ref[1], rts, bhs)

    md = each(lambda t: t * msk_ref[2], m)
    p2 = each(lambda t: _dot(t, t), md)
    p4 = each(lambda t: _dot(t, t), p2)
    p8 = each(lambda t: _dot(t, t), p4)
    xinv = each(lambda t: msk_ref[5] - t, md)
    for pw in (p2, p4, p8):
        xinv = each(lambda t, u: t + _dot(t, u), xinv, pw)
    for lvl in (3, 4):
        off = each(lambda t, u: _dot(t, u * msk_ref[lvl]), xinv, m)
        xinv = each(lambda t, u: t - _dot(u, t), xinv, off)

    ht = [ht_scr[n] for n in bs]
    wmat = each(lambda t, u, c, d: _dot_nt(t, u) + _dot(c, d), kts, ht, akk, vs)
    us = each(_dot, xinv, wmat)
    ys = each(lambda t, u, c, d, f, w: _dot_nt(t, u) + _dot(c, d) - _dot(f, w), rts, ht, ark, vs, arb, us)
    y = [t[0:L] + t[L:2 * L] + t[2 * L:3 * L] + t[3 * L:4 * L] for t in ys]

    lhs = each(lambda t, u: jnp.concatenate([t, u.astype(BF16)], axis=0), vs, us)
    rhs = each(lambda t, u, c, d: jnp.concatenate([stack(t * c * d), stack(-(u * c * d))], axis=0),
               k2, beta, ginv, g_last)
    hnew = each(lambda t, u, c, d: t * u + _dot_tn(c, d) * e32, ht, g_last, lhs, rhs)
    for n in bs:
        ht_scr[n] = hnew[n]

    d = each(lambda t: t - _dot_sel2(t, e) * (1.0 / HEAD_DIM), y)
    yv = each(lambda t: _dot_sel2(t * t, e) * (1.0 / HEAD_DIM), d)
    yn = each(lambda t, u: t * lax.rsqrt(u + GN_EPS) * lg_ref[...] + lb_ref[...], d, yv)
    bonus = each(lambda t, u, c: _dot_sel2(t * u * rk_ref[...], e) * c, r, k2, v)
    for n in bs:
        o_ref[n] = (yn[n] + bonus[n]) * g[n]


def _rwkv_consts(chunk):
    n = N_HEADS * chunk
    idx = jnp.arange(n)
    rh, rt = idx // chunk, idx % chunk
    same = rh[:, None] == rh[None, :]
    blk = lambda s: same & ((rt[:, None] // s) == (rt[None, :] // s))
    masks = jnp.stack([
        same & (rt[:, None] > rt[None, :]),
        same & (rt[:, None] >= rt[None, :]),
        blk(16),
        blk(32) & ~blk(16),
        blk(64) & ~blk(32),
        jnp.eye(n, dtype=bool),
    ]).astype(F32)
    lane_head = jnp.arange(BW) // HEAD_DIM
    e = (lane_head[:, None] == lane_head[None, :]).astype(BF16)
    hm = (jnp.arange(N_HEADS)[:, None] == lane_head[None, :]).astype(F32).reshape(N_HEADS, 1, BW)
    tri = (jnp.arange(chunk)[:, None] >= jnp.arange(chunk)[None, :]).astype(BF16)
    return e, tri, masks, hm


def _hi_lo(w):
    hi = w.astype(BF16)
    return jnp.stack([hi, (w - hi.astype(F32)).astype(BF16)])


def _rwkv(rw, params, consts, bsz, seq, chunk, nb):
    nc = seq // chunk
    n = N_HEADS * chunk
    mu, w0, w2p, a0, a2p, g2, k_k, k_a, r_k, lg, lb = params
    e, tri, masks, hm = consts
    c2 = lambda shape: pl.BlockSpec(shape, lambda b, c: (0, 0))
    c3 = lambda shape: pl.BlockSpec(shape, lambda b, c: (0, 0, 0))
    out = pl.pallas_call(
        functools.partial(_rwkv_kernel, chunk=chunk, nb=nb),
        grid=(bsz // nb, nc),
        in_specs=[pl.BlockSpec((nb, chunk, RWKV_PROJ), lambda b, c: (b, c, 0)),
                  c2((1, RWKV_PROJ)), c2((1, BW)), c3((2, LANE, BW)), c2((1, BW)), c3((2, LANE, BW)),
                  c3((2, LANE, BW)), c2((1, BW)), c2((1, BW)), c2((1, BW)), c2((1, BW)), c2((1, BW)),
                  c2((BW, BW)), c2((chunk, chunk)), c3((6, n, n)), c3((N_HEADS, 1, BW))],
        out_specs=pl.BlockSpec((nb, chunk, BW), lambda b, c: (b, c, 0)),
        out_shape=jax.ShapeDtypeStruct((bsz, seq, BW), F32),
        scratch_shapes=[pltpu.VMEM((nb, BW, BW), F32), pltpu.VMEM((nb, 8, RWKV_PROJ), F32)],
        compiler_params=_cp(("parallel", "arbitrary")),
        name="rwkv7_mix",
    )(rw.reshape(bsz, seq, RWKV_PROJ), mu, w0, _hi_lo(w2p), a0, _hi_lo(a2p), _hi_lo(g2),
      k_k, k_a, r_k, lg, lb, e, tri, masks, hm)
    return out.reshape(bsz * seq, BW)


def _merge_kernel(h_ref, y0_ref, y1_ref, y2_ref, y3_ref, wg_ref, wb_ref, wo_ref, g_ref, b_ref, o_ref):
    h = h_ref[...]
    hb = h.astype(BF16)
    merged = None
    for n, y_ref in enumerate((y0_ref, y1_ref, y2_ref, y3_ref)):
        gate = jax.nn.sigmoid(jnp.dot(hb, wg_ref[:, n * D_MODEL:(n + 1) * D_MODEL],
                                      preferred_element_type=F32))
        up = jnp.dot(y_ref[...].astype(BF16), wb_ref[n], preferred_element_type=F32)
        merged = gate * up if merged is None else merged + gate * up
    z = ALPHA * h + jnp.dot(merged.astype(BF16), wo_ref[...], preferred_element_type=F32)
    o_ref[...] = _layer_norm(z, g_ref[...], b_ref[...])


def _merge(h, y_s5, y_fox, y_rwkv, y_sb, wg, wb, wo, g, b, bsz, seq, tm):
    ns = seq // tm
    t = bsz * seq
    row = lambda bi, i: (bi * ns + i, 0)
    return pl.pallas_call(
        _merge_kernel,
        grid=(bsz, ns),
        in_specs=[pl.BlockSpec((tm, D_MODEL), row),
                  pl.BlockSpec((tm, BW), lambda bi, i: (i, bi)),
                  pl.BlockSpec((tm, BW), row), pl.BlockSpec((tm, BW), row), pl.BlockSpec((tm, BW), row),
                  pl.BlockSpec((D_MODEL, N_BRANCHES * D_MODEL), lambda bi, i: (0, 0)),
                  pl.BlockSpec((N_BRANCHES, BW, D_MODEL), lambda bi, i: (0, 0, 0)),
                  pl.BlockSpec((D_MODEL, D_MODEL), lambda bi, i: (0, 0)),
                  pl.BlockSpec((1, D_MODEL), lambda bi, i: (0, 0)),
                  pl.BlockSpec((1, D_MODEL), lambda bi, i: (0, 0))],
        out_specs=pl.BlockSpec((tm, D_MODEL), row),
        out_shape=jax.ShapeDtypeStruct((t, D_MODEL), F32),
        compiler_params=_cp(("parallel", "parallel")),
        name="merge_ln1",
    )(h, y_s5, y_fox, y_rwkv, y_sb, wg, wb, wo, g, b)


def _mlp_kernel(h_ref, p_ref, w1_ref, w2_ref, pw_ref, pg_ref, g_ref, b_ref, o_ref, *, ff_chunk):
    h = h_ref[...]
    hb = h.astype(BF16)
    ffn = None
    for c in range(D_FF // ff_chunk):
        sl = slice(c * ff_chunk, (c + 1) * ff_chunk)
        hid = jnp.maximum(jnp.dot(hb, w1_ref[:, sl], preferred_element_type=F32), 0.0)
        part = jnp.dot((hid * hid).astype(BF16), w2_ref[sl, :], preferred_element_type=F32)
        ffn = part if ffn is None else ffn + part
    ple = (jax.nn.sigmoid(jnp.dot(hb, pg_ref[...], preferred_element_type=F32))
           * jnp.dot(p_ref[...].astype(BF16), pw_ref[...], preferred_element_type=F32))
    o_ref[...] = _layer_norm(ALPHA * h + ffn + ple, g_ref[...], b_ref[...])


def _mlp(h, p, w1, w2, pw, pg, g, b, tm):
    t = h.shape[0]
    full = lambda shape: pl.BlockSpec(shape, lambda i: (0, 0))
    return pl.pallas_call(
        functools.partial(_mlp_kernel, ff_chunk=1024),
        grid=(t // tm,),
        in_specs=[pl.BlockSpec((tm, D_MODEL), lambda i: (i, 0)),
                  pl.BlockSpec((tm, PLE_DIM), lambda i: (i, 0)),
                  full((D_MODEL, D_FF)), full((D_FF, D_MODEL)), full((PLE_DIM, D_MODEL)),
                  full((D_MODEL, D_MODEL)), full((1, D_MODEL)), full((1, D_MODEL))],
        out_specs=pl.BlockSpec((tm, D_MODEL), lambda i: (i, 0)),
        out_shape=jax.ShapeDtypeStruct((t, D_MODEL), F32),
        compiler_params=_cp(("parallel",)),
        name="mlp_ple_ln2",
    )(h, p, w1, w2, pw, pg, g, b)


def _pad_rows(m, rows, at):
    out = jnp.zeros((rows, m.shape[1]), m.dtype)
    return out.at[at:at + m.shape[0]].set(m)


def kernel(x, p, w_in, s5_lambda_re, s5_lambda_im, s5_log_dt, s5_b_re, s5_b_im, s5_c_re, s5_c_im, s5_d, s5_glu_w, s5_glu_b, fox_f_bias, rwkv_mu, rwkv_w0, rwkv_w2, rwkv_a0, rwkv_a2, rwkv_g2, rwkv_k_k, rwkv_k_a, rwkv_r_k, rwkv_lnx_g, rwkv_lnx_b, w_branch, w_out, ln1_g, ln1_b, mlp_w1, mlp_w2, ple_w, ple_gate_w, ln2_g, ln2_b):
    bsz, seq, _ = x.shape
    t = bsz * seq
    tm = min(512, seq)
    ts = min(128, seq)
    chunk = RWKV_CHUNK
    scale = HEAD_DIM ** -0.5 * LOG2E

    rwkv_consts = _rwkv_consts(chunk)
    tri_lane = (jnp.arange(LANE)[:, None] <= jnp.arange(LANE)[None, :]).astype(F32)
    later_key = (jnp.arange(ATT_TILE)[:, None] < jnp.arange(ATT_TILE)[None, :]).astype(BF16)
    head_rows = (jnp.arange(8)[:, None] == jnp.arange(BW)[None, :] // HEAD_DIM).astype(BF16)
    row2 = lambda a: a.reshape(1, -1)

    h = x.reshape(t, D_MODEL)
    for i in range(DEPTH):
        w = w_in[i]
        w_main = jnp.concatenate([
            w[:, 0:256], w[:, 256:512] * scale, w[:, 512:768],
            w[:, 1028:2052],
            w[:, 2052:2308] * scale, w[:, 2308:2564]], axis=1).astype(BF16)
        w_vt = jnp.concatenate([w[:, 768:1024].T, w[:, 2564:2820].T], axis=0).astype(BF16)
        w_ff = _pad_rows(w[:, 1024:1028].T, 8, 0).astype(BF16)
        w_gates = w[:, 2820:].astype(BF16)

        u_sb, fqk, rw, sqk, fvt, svt, fft = _proj(h, w_main, w_vt, w_ff, bsz, seq, tm)

        bbar, cmat, lam = _s5_params(s5_lambda_re[i], s5_lambda_im[i], s5_log_dt[i], s5_b_re[i],
                                     s5_b_im[i], s5_c_re[i], s5_c_im[i], bsz)
        y_s5 = _s5(u_sb.reshape(seq * bsz, BW), bbar, cmat, lam, row2(s5_d[i]),
                   s5_glu_w[i].astype(BF16), row2(s5_glu_b[i]), bsz, seq, ts)
        y_s5 = y_s5.reshape(seq, bsz * BW)

        bias8 = _pad_rows(fox_f_bias[i].reshape(N_HEADS, 1), 8, 0)
        c_row = _fox_c(fft, bias8, tri_lane)
        ccol = jnp.transpose(c_row[:, :N_HEADS], (0, 2, 1)).reshape(t, N_HEADS)
        y_fox = _fox_attn(fqk, fvt, ccol, c_row, head_rows, bsz, seq)

        rparams = (row2(rwkv_mu[i]), row2(rwkv_w0[i]), _pad_rows(rwkv_w2[i], LANE, 0),
                   row2(rwkv_a0[i]), _pad_rows(rwkv_a2[i], LANE, 64), rwkv_g2[i],
                   row2(rwkv_k_k[i]), row2(rwkv_k_a[i]), row2(rwkv_r_k[i]),
                   row2(rwkv_lnx_g[i]), row2(rwkv_lnx_b[i]))
        y_rwkv = _rwkv(rw, rparams, rwkv_consts, bsz, seq, chunk, 4 if bsz % 4 == 0 else 1)

        y_sb = _sb_attn(sqk, svt, later_key, bsz, seq)

        h = _merge(h, y_s5, y_fox, y_rwkv, y_sb, w_gates, w_branch[i].astype(BF16),
                   w_out[i].astype(BF16), row2(ln1_g[i]), row2(ln1_b[i]), bsz, seq, tm)
        h = _mlp(h, p[i].reshape(t, PLE_DIM), mlp_w1[i].astype(BF16), mlp_w2[i].astype(BF16),
                 ple_w[i].astype(BF16), ple_gate_w[i].astype(BF16), row2(ln2_g[i]), row2(ln2_b[i]), tm)
    return h.reshape(bsz, seq, D_MODEL)
```

```python
import functools
import math

import jax
import jax.numpy as jnp
from jax import lax
from jax.experimental import pallas as pl
from jax.experimental.pallas import tpu as pltpu

F32 = jnp.float32
BF16 = jnp.bfloat16
HI = lax.Precision.HIGHEST

D_MODEL = 1024
PLE_DIM = 256
N_BRANCHES = 4
BW = 256
HEAD_DIM = 64
N_HEADS = 4
S5_GROUP_CH = 16
S5_GROUPS = 16
S5_STATE = 64
S5_N = S5_GROUPS * S5_STATE
RWKV_PROJ = 1024
D_FF = 4096
LN_EPS = 1e-5
GN_EPS = 64e-5
DEPTH = 2
ALPHA = (2 * DEPTH) ** 0.25
LOG2E = math.log2(math.e)
LANE = 128
RWKV_CHUNK = 64
ATT_TILE = 256
SB_DEAD = 160.0
FOX_DEAD = -152.0
VMEM_LIMIT = 56 * 1024 * 1024


def _cp(sem, vmem=VMEM_LIMIT):
    return pltpu.CompilerParams(dimension_semantics=sem, vmem_limit_bytes=vmem)


def _dot(a, b):
    return jnp.dot(a.astype(BF16), b.astype(BF16), preferred_element_type=F32)


def _dot_hi(a, b):
    return jnp.dot(a, b, precision=HI, preferred_element_type=F32)


def _dot_nt(a, b):
    return lax.dot_general(a.astype(BF16), b.astype(BF16), (((1,), (1,)), ((), ())),
                           preferred_element_type=F32)


def _dot_tn(a, b):
    return lax.dot_general(a.astype(BF16), b.astype(BF16), (((0,), (0,)), ((), ())),
                           preferred_element_type=F32)


def _softplus(x):
    return jnp.maximum(x, 0.0) + jnp.log1p(jnp.exp(-jnp.abs(x)))


def _gelu_tanh(x):
    return 0.5 * x * (1.0 + jnp.tanh(math.sqrt(2.0 / math.pi) * (x + 0.044715 * (x * x * x))))


def _layer_norm(z, g, b):
    mu = jnp.mean(z, axis=-1, keepdims=True)
    d = z - mu
    var = jnp.mean(d * d, axis=-1, keepdims=True)
    return d * lax.rsqrt(var + LN_EPS) * g + b


def _proj_kernel(h_ref, wm_ref, wvt_ref, wf_ref, u_ref, fqk_ref, rw_ref, sqk_ref, fvt_ref, svt_ref,
                 fft_ref, *, tm):
    hb = h_ref[...].astype(BF16)
    u_ref[...] = jnp.dot(hb, wm_ref[:, 0:256], preferred_element_type=F32)
    fqk_ref[...] = jnp.dot(hb, wm_ref[:, 256:768], preferred_element_type=F32).astype(BF16)
    rw_ref[...] = jnp.dot(hb, wm_ref[:, 768:1792], preferred_element_type=F32)
    sqk_ref[...] = jnp.dot(hb, wm_ref[:, 1792:2304], preferred_element_type=F32).astype(BF16)
    vt = _dot_nt(wvt_ref[...], hb).astype(BF16)
    for jj in range(tm // ATT_TILE):
        sl = slice(jj * ATT_TILE, (jj + 1) * ATT_TILE)
        fvt_ref[0, jj] = vt[0:BW, sl]
        svt_ref[0, jj] = vt[BW:2 * BW, sl]
    fft_ref[0] = _dot_nt(wf_ref[...], hb)


def _proj(h, wm, wvt, wf, bsz, seq, tm):
    ns = seq // tm
    t = bsz * seq
    row = lambda b, i: (b * ns + i, 0)
    vt_spec = pl.BlockSpec((1, tm // ATT_TILE, BW, ATT_TILE), lambda b, i: (b, i, 0, 0))
    vt_shape = jax.ShapeDtypeStruct((bsz, seq // ATT_TILE, BW, ATT_TILE), BF16)
    return pl.pallas_call(
        functools.partial(_proj_kernel, tm=tm),
        grid=(bsz, ns),
        in_specs=[pl.BlockSpec((tm, D_MODEL), row),
                  pl.BlockSpec((D_MODEL, 2304), lambda b, i: (0, 0)),
                  pl.BlockSpec((2 * BW, D_MODEL), lambda b, i: (0, 0)),
                  pl.BlockSpec((8, D_MODEL), lambda b, i: (0, 0))],
        out_specs=[pl.BlockSpec((tm, BW), lambda b, i: (i, b)),
                   pl.BlockSpec((tm, 2 * BW), row),
                   pl.BlockSpec((tm, RWKV_PROJ), row),
                   pl.BlockSpec((tm, 2 * BW), row),
                   vt_spec, vt_spec,
                   pl.BlockSpec((1, 8, tm), lambda b, i: (b, 0, i))],
        out_shape=[jax.ShapeDtypeStruct((seq, bsz * BW), F32),
                   jax.ShapeDtypeStruct((t, 2 * BW), BF16),
                   jax.ShapeDtypeStruct((t, RWKV_PROJ), F32),
                   jax.ShapeDtypeStruct((t, 2 * BW), BF16),
                   vt_shape, vt_shape,
                   jax.ShapeDtypeStruct((bsz, 8, seq), F32)],
        compiler_params=_cp(("parallel", "parallel")),
        name="in_proj",
    )(h, wm, wvt, wf)


def _s5_kernel(u_ref, bbar_ref, cmat_ref, lam_ref, d_ref, gw_ref, gb_ref, y_ref, xs_scr, st_scr,
               *, ts, bsz):
    @pl.when(pl.program_id(0) == 0)
    def _():
        st_scr[...] = jnp.zeros(st_scr.shape, F32)

    u = u_ref[...]
    xs_scr[...] = jnp.dot(u.astype(BF16), bbar_ref[...], preferred_element_type=F32)
    lr = lam_ref[:, 0:S5_N]
    li = lam_ref[:, S5_N:2 * S5_N]

    def body(t, carry):
        xr, xi = carry
        r0 = pl.multiple_of(t * bsz, bsz)
        br = xs_scr[pl.ds(r0, bsz), 0:S5_N]
        bi = xs_scr[pl.ds(r0, bsz), S5_N:2 * S5_N]
        nr = lr * xr - li * xi + br
        ni = lr * xi + li * xr + bi
        xs_scr[pl.ds(r0, bsz), 0:S5_N] = nr
        xs_scr[pl.ds(r0, bsz), S5_N:2 * S5_N] = ni
        return nr, ni

    xr, xi = lax.fori_loop(0, ts, body, (st_scr[:, 0:S5_N], st_scr[:, S5_N:2 * S5_N]), unroll=4)
    st_scr[:, 0:S5_N] = xr
    st_scr[:, S5_N:2 * S5_N] = xi

    y = jnp.dot(xs_scr[...].astype(BF16), cmat_ref[...], preferred_element_type=F32) + d_ref[...] * u
    y = _gelu_tanh(y)
    y_ref[...] = y * jax.nn.sigmoid(_dot(y, gw_ref[...]) + gb_ref[...])


def _s5(u_sb, bbar, cmat, lam, dvec, gw, gb, bsz, seq, ts):
    rows = ts * bsz
    full = lambda shape: pl.BlockSpec(shape, lambda i: (0, 0))
    return pl.pallas_call(
        functools.partial(_s5_kernel, ts=ts, bsz=bsz),
        grid=(seq // ts,),
        in_specs=[pl.BlockSpec((rows, BW), lambda i: (i, 0)),
                  full((BW, 2 * S5_N)), full((2 * S5_N, BW)), full((bsz, 2 * S5_N)),
                  full((1, BW)), full((BW, BW)), full((1, BW))],
        out_specs=pl.BlockSpec((rows, BW), lambda i: (i, 0)),
        out_shape=jax.ShapeDtypeStruct((seq * bsz, BW), F32),
        scratch_shapes=[pltpu.VMEM((rows, 2 * S5_N), F32), pltpu.VMEM((bsz, 2 * S5_N), F32)],
        compiler_params=_cp(("arbitrary",)),
        name="s5_mixer",
    )(u_sb, bbar, cmat, lam, dvec, gw, gb)


def _s5_params(lam_re, lam_im, log_dt, b_re, b_im, c_re, c_im, bsz):
    dt = jnp.exp(log_dt)[:, None]
    mag = jnp.exp(lam_re * dt)
    ang = lam_im * dt
    lb_re = mag * jnp.cos(ang)
    lb_im = mag * jnp.sin(ang)
    den = jnp.square(lam_re) + jnp.square(lam_im)
    nr = lb_re - 1.0
    f_re = (nr * lam_re + lb_im * lam_im) / den
    f_im = (lb_im * lam_re - nr * lam_im) / den
    bb_re = f_re[..., None] * b_re - f_im[..., None] * b_im
    bb_im = f_re[..., None] * b_im + f_im[..., None] * b_re
    eye = jnp.eye(S5_GROUPS, dtype=F32)
    bd_in = lambda m: jnp.einsum('gph,gk->ghkp', m, eye).reshape(BW, S5_N)
    bd_out = lambda m: jnp.einsum('ghp,gk->gpkh', m, eye).reshape(S5_N, BW)
    bbar = jnp.concatenate([bd_in(bb_re), bd_in(bb_im)], axis=1).astype(BF16)
    cmat = jnp.concatenate([bd_out(c_re), -bd_out(c_im)], axis=0).astype(BF16)
    lam = jnp.concatenate([lb_re.reshape(1, S5_N), lb_im.reshape(1, S5_N)], axis=1)
    return bbar, cmat, jnp.broadcast_to(lam, (bsz, 2 * S5_N))


def _foxc_kernel(ff_ref, bias_ref, tri_ref, c_ref, *, n_chunks):
    carry = jnp.zeros((8, 1), F32)
    for ch in range(n_chunks):
        sl = slice(ch * LANE, (ch + 1) * LANE)
        x = ff_ref[0, :, sl] + bias_ref[...]
        lf = jnp.minimum(x, 0.0) - jnp.log1p(jnp.exp(-jnp.abs(x)))
        cs = _dot_hi(lf, tri_ref[...]) + carry
        c_ref[0, :, sl] = cs * LOG2E
        carry = cs[:, LANE - 1:LANE]


def _fox_c(fft, bias8, tri):
    bsz, _, seq = fft.shape
    return pl.pallas_call(
        functools.partial(_foxc_kernel, n_chunks=seq // LANE),
        grid=(bsz,),
        in_specs=[pl.BlockSpec((1, 8, seq), lambda b: (b, 0, 0)),
                  pl.BlockSpec((8, 1), lambda b: (0, 0)),
                  pl.BlockSpec((LANE, LANE), lambda b: (0, 0))],
        out_specs=pl.BlockSpec((1, 8, seq), lambda b: (b, 0, 0)),
        out_shape=jax.ShapeDtypeStruct((bsz, 8, seq), F32),
        compiler_params=_cp(("parallel",)),
        name="fox_cumlogf",
    )(fft, bias8, tri)


def _head_select(x, hh):
    lane = lax.broadcasted_iota(jnp.int32, x.shape, 1)
    sel = (lane < HEAD_DIM) if hh == 0 else (lane >= HEAD_DIM)
    return jnp.where(sel, x, jnp.zeros_like(x))


def _pair_cols(h):
    return slice((h // 2) * LANE, (h // 2 + 1) * LANE)


def _head_rows(h):
    return slice(h * HEAD_DIM, (h + 1) * HEAD_DIM)


def _split_heads(q):
    return [_head_select(q[:, _pair_cols(h)], h % 2) for h in range(N_HEADS)]


def _tile_positions():
    kpos = lax.broadcasted_iota(jnp.int32, (ATT_TILE, ATT_TILE), 0)
    qpos = lax.broadcasted_iota(jnp.int32, (ATT_TILE, ATT_TILE), 1)
    return kpos, qpos


def _scores(k_ref, j, qh):
    ks = k_ref[pl.ds(pl.multiple_of(j * ATT_TILE, ATT_TILE), ATT_TILE), :]
    return tuple(_dot_nt(ks[:, _pair_cols(h)], qh[h]) for h in range(N_HEADS))

def _head_sq_norms(esel, x):
    xf = x.astype(F32)
    hi, lo = _split2(xf * xf)
    return _dot_nt(esel, hi) + _dot_nt(esel, lo)


def _fox_kernel(q_ref, k_ref, vt_ref, ck_ref, cq_ref, esel_ref, o_ref, acc_scr, st_scr, s_scr, kn_scr,
                *, n_tiles):
    i = pl.program_id(1)
    q = q_ref[...]
    qh = _split_heads(q)
    cq = [cq_ref[0, h:h + 1, :] for h in range(N_HEADS)]
    esel = esel_ref[...]
    hs = range(N_HEADS)
    acc_scr[...] = jnp.zeros(acc_scr.shape, F32)
    for h in hs:
        st_scr[2 * h:2 * h + 1, :] = jnp.full((1, ATT_TILE), -1e30, F32)
        st_scr[2 * h + 1:2 * h + 2, :] = jnp.zeros((1, ATT_TILE), F32)

    @pl.when(i == 0)
    def _():
        def widen(j, mx):
            ks = k_ref[pl.ds(pl.multiple_of(j * ATT_TILE, ATT_TILE), ATT_TILE), :]
            return jnp.maximum(mx, _head_sq_norms(esel, ks))
        mx = lax.fori_loop(0, n_tiles, widen, jnp.zeros((8, ATT_TILE), F32))
        kn_scr[...] = jnp.broadcast_to(jnp.max(mx, axis=1, keepdims=True), kn_scr.shape)

    qn = _head_sq_norms(esel, q)
    reach = [jnp.sqrt(qn[h:h + 1, :] * kn_scr[h:h + 1, 0:1]) * 1.001 + 1.0 + cq[h] for h in hs]

    def tile(j, s, masked):
        k0 = pl.multiple_of(j * ATT_TILE, ATT_TILE)
        vt = vt_ref[0, j]
        m_old = [st_scr[2 * h:2 * h + 1, :] for h in hs]
        l_old = [st_scr[2 * h + 1:2 * h + 2, :] for h in hs]
        t1 = [s[h] - ck_ref[pl.ds(k0, ATT_TILE), h:h + 1] for h in hs]
        if masked:
            kpos, qpos = _tile_positions()
            t1 = [jnp.where(kpos <= qpos, t, -1e30) for t in t1]
        m_new = [jnp.maximum(m_old[h], jnp.max(t1[h], axis=0, keepdims=True) + cq[h]) for h in hs]
        alpha = [jnp.exp2(m_old[h] - m_new[h]) for h in hs]
        p = [jnp.exp2(t1[h] - (m_new[h] - cq[h])) for h in hs]
        l = [alpha[h] * l_old[h] + jnp.sum(p[h], axis=0, keepdims=True) for h in hs]
        pv = [jnp.dot(vt[_head_rows(h), :], p[h].astype(BF16), preferred_element_type=F32) for h in hs]
        for h in hs:
            acc_scr[h] = alpha[h] * acc_scr[h] + pv[h]
            st_scr[2 * h:2 * h + 1, :] = m_new[h]
            st_scr[2 * h + 1:2 * h + 2, :] = l[h]

    def held():
        return tuple(s_scr[h] for h in hs)

    def hold(s):
        for h in hs:
            s_scr[h] = s[h]

    def all_dead(newest_unvisited):
        row = jnp.maximum(newest_unvisited, 0)
        worst = None
        for h in hs:
            gap = reach[h] - ck_ref[pl.ds(row, 1), h:h + 1] - st_scr[2 * h:2 * h + 1, :]
            worst = gap if worst is None else jnp.maximum(worst, gap)
        return jnp.max(worst) < FOX_DEAD

    def pair(carry):
        jj, _ = carry
        first = i - 1 - 2 * jj
        rest_dead = all_dead((first - 1) * ATT_TILE - 1)
        s_first = held()
        s_second = _scores(k_ref, first - 1, qh)
        tile(first, s_first, False)
        hold(_scores(k_ref, jnp.maximum(first - 2, 0), qh))
        tile(first - 1, s_second, False)
        return jj + 1, rest_dead

    s_diag = _scores(k_ref, i, qh)
    hold(_scores(k_ref, jnp.maximum(i - 1, 0), qh))
    tile(i, s_diag, True)
    _, dead = lax.while_loop(lambda c: jnp.logical_and(c[0] < i // 2, jnp.logical_not(c[1])),
                             pair, (jnp.int32(0), all_dead(i * ATT_TILE - 1)))

    @pl.when(jnp.logical_and(lax.rem(i, 2) == 1, jnp.logical_not(dead)))
    def _():
        tile(0, held(), False)

    out_t = jnp.concatenate([acc_scr[h] / st_scr[2 * h + 1:2 * h + 2, :] for h in hs], axis=0)
    o_ref[...] = out_t.T


def _att_specs(seq, nq):
    return [pl.BlockSpec((ATT_TILE, BW), lambda b, i: (b * nq + i, 0)),
            pl.BlockSpec((seq, BW), lambda b, i: (b, 1)),
            pl.BlockSpec((1, seq // ATT_TILE, BW, ATT_TILE), lambda b, i: (b, 0, 0, 0))]


def _fox_attn(qk, vt, ccol, crow, esel, bsz, seq):
    nq = seq // ATT_TILE
    return pl.pallas_call(
        functools.partial(_fox_kernel, n_tiles=nq),
        grid=(bsz, nq),
        in_specs=_att_specs(seq, nq) + [
            pl.BlockSpec((seq, N_HEADS), lambda b, i: (b, 0)),
            pl.BlockSpec((1, 8, ATT_TILE), lambda b, i: (b, 0, i)),
            pl.BlockSpec((8, BW), lambda b, i: (0, 0))],
        out_specs=pl.BlockSpec((ATT_TILE, BW), lambda b, i: (b * nq + i, 0)),
        out_shape=jax.ShapeDtypeStruct((bsz * seq, BW), F32),
        scratch_shapes=[pltpu.VMEM((N_HEADS, HEAD_DIM, ATT_TILE), F32),
                        pltpu.VMEM((2 * N_HEADS, ATT_TILE), F32),
                        pltpu.VMEM((N_HEADS, ATT_TILE, ATT_TILE), F32),
                        pltpu.VMEM((8, LANE), F32)],
        compiler_params=_cp(("parallel", "arbitrary")),
        name="fox_attention",
    )(qk, qk, vt, ccol, crow, esel)


def _sb_kernel(q_ref, k_ref, vt_ref, w_ref, o_ref, acc_scr, st_scr, s_scr):
    i = pl.program_id(1)
    qh = _split_heads(q_ref[...])
    wmat = w_ref[...]
    hs = range(N_HEADS)
    acc_scr[...] = jnp.zeros(acc_scr.shape, F32)
    st_scr[...] = jnp.zeros(st_scr.shape, F32)

    def tile(j, z, masked):
        vt = vt_ref[0, j]
        seen = [st_scr[h:h + 1, :] for h in hs]
        sp = [jnp.maximum(t, 0.0) + jnp.log(1.0 + jnp.exp2(-jnp.abs(t))) * LOG2E for t in z]
        if masked:
            kpos, qpos = _tile_positions()
            mask = kpos < qpos
            sp = [jnp.where(mask, t, 0.0) for t in sp]
        later = [jnp.dot(wmat, t.astype(BF16), preferred_element_type=F32) for t in sp]
        w = [jnp.exp2((z[h] - sp[h]) - later[h] - seen[h]) for h in hs]
        if masked:
            w = [jnp.where(mask, t, 0.0) for t in w]
        pv = [jnp.dot(vt[_head_rows(h), :], w[h].astype(BF16), preferred_element_type=F32) for h in hs]
        for h in hs:
            acc_scr[h] = acc_scr[h] + pv[h]
            st_scr[h:h + 1, :] = seen[h] + jnp.sum(sp[h], axis=0, keepdims=True)

    def held():
        return tuple(s_scr[h] for h in hs)

    def hold(z):
        for h in hs:
            s_scr[h] = z[h]

    def all_dead():
        return jnp.min(st_scr[0:N_HEADS, :]) >= SB_DEAD

    def pair(carry):
        jj, _ = carry
        first = i - 2 - 2 * jj
        z_first = held()
        z_second = _scores(k_ref, first - 1, qh)
        tile(first, z_first, False)
        hold(_scores(k_ref, jnp.maximum(first - 2, 0), qh))
        tile(first - 1, z_second, False)
        return jj + 1, all_dead()

    z_diag = _scores(k_ref, i, qh)
    hold(_scores(k_ref, jnp.maximum(i - 1, 0), qh))
    tile(i, z_diag, True)

    @pl.when(i >= 1)
    def _():
        z_prev = held()
        hold(_scores(k_ref, jnp.maximum(i - 2, 0), qh))
        tile(i - 1, z_prev, False)

    rest = jnp.maximum(i - 1, 0)
    _, dead = lax.while_loop(lambda c: jnp.logical_and(c[0] < rest // 2, jnp.logical_not(c[1])),
                             pair, (jnp.int32(0), all_dead()))

    @pl.when(jnp.logical_and(lax.rem(rest, 2) == 1, jnp.logical_not(dead)))
    def _():
        tile(0, held(), False)

    o_ref[...] = jnp.concatenate([acc_scr[h] for h in hs], axis=0).T


def _sb_attn(qk, vt, wmat, bsz, seq):
    nq = seq // ATT_TILE
    return pl.pallas_call(
        _sb_kernel,
        grid=(bsz, nq),
        in_specs=_att_specs(seq, nq) + [pl.BlockSpec((ATT_TILE, ATT_TILE), lambda b, i: (0, 0))],
        out_specs=pl.BlockSpec((ATT_TILE, BW), lambda b, i: (b * nq + i, 0)),
        out_shape=jax.ShapeDtypeStruct((bsz * seq, BW), F32),
        scratch_shapes=[pltpu.VMEM((N_HEADS, HEAD_DIM, ATT_TILE), F32),
                        pltpu.VMEM((8, ATT_TILE), F32),
                        pltpu.VMEM((N_HEADS, ATT_TILE, ATT_TILE), F32)],
        compiler_params=_cp(("parallel", "parallel")),
        name="sb_attention",
    )(qk, qk, vt, wmat)


def _split2(x):
    hi = x.astype(BF16)
    return hi, (x - hi.astype(F32)).astype(BF16)


def _dot_w3(x, w_ref):
    hi, lo = _split2(x)
    return (jnp.dot(hi, w_ref[0], preferred_element_type=F32)
            + jnp.dot(lo, w_ref[0], preferred_element_type=F32)
            + jnp.dot(hi, w_ref[1], preferred_element_type=F32))


def _dot_sel2(x, sel):
    hi, lo = _split2(x)
    return jnp.dot(hi, sel, preferred_element_type=F32) + jnp.dot(lo, sel, preferred_element_type=F32)


def _dot_sel3_left(sel, x):
    hi = x.astype(BF16)
    r1 = x - hi.astype(F32)
    mid = r1.astype(BF16)
    lo = (r1 - mid.astype(F32)).astype(BF16)
    return (jnp.dot(sel, hi, preferred_element_type=F32) + jnp.dot(sel, mid, preferred_element_type=F32)
            + jnp.dot(sel, lo, preferred_element_type=F32))


def _rwkv_kernel(rw_ref, mu_ref, w0_ref, w2_ref, a0_ref, a2_ref, g2_ref, kk_ref, ka_ref, rk_ref,
                 lg_ref, lb_ref, e_ref, tri_ref, msk_ref, hm_ref, o_ref, ht_scr, last_scr,
                 *, chunk, nb):
    L = chunk

    @pl.when(pl.program_id(1) == 0)
    def _():
        ht_scr[...] = jnp.zeros(ht_scr.shape, F32)
        last_scr[...] = jnp.zeros(last_scr.shape, F32)

    e = e_ref[...]
    e32 = e.astype(F32)
    bs = range(nb)

    def stack(t):
        return jnp.concatenate([t * hm_ref[h] for h in range(N_HEADS)], axis=0).astype(BF16)

    def each(f, *lists):
        return [f(*args) for args in zip(*lists)]

    def shift_mix(n):
        x = rw_ref[n]
        row = lax.broadcasted_iota(jnp.int32, x.shape, 0)
        prev = jnp.where(row == 0, last_scr[n, 0:1, :], pltpu.roll(x, 1, axis=0))
        last_scr[n, 0:1, :] = x[L - 1:L, :]
        return x + (prev - x) * mu_ref[...]

    xs = each(shift_mix, bs)
    r = [t[:, 0:256] for t in xs]
    k = [t[:, 256:512] for t in xs]
    v = [t[:, 512:768] for t in xs]
    wa = [t[:, 768:896] for t in xs]
    g1 = [t[:, 896:1024] for t in xs]
    wpre = each(lambda t: w0_ref[...] + _dot_w3(jnp.tanh(t), w2_ref), wa)
    ld = each(lambda t: -jnp.exp(-_softplus(-t) - 0.5), wpre)
    a = each(lambda t: jax.nn.sigmoid(a0_ref[...] + _dot_w3(t, a2_ref)), wa)
    g = each(lambda t: _dot_w3(jax.nn.sigmoid(t), g2_ref), g1)
    kk = each(lambda t: t * kk_ref[...], k)
    kap = each(lambda t: t / jnp.maximum(jnp.sqrt(_dot_sel2(t * t, e)), 1e-12), kk)
    k2 = each(lambda t, u: t * (1.0 + (u - 1.0) * ka_ref[...]), k, a)
    beta = each(lambda t, u: t * u, kap, a)

    cs = each(lambda t: _dot_sel3_left(tri_ref[...], t), ld)
    gam = each(jnp.exp, cs)
    ginv = each(lambda t: jnp.exp(-t), cs)
    g_last = [t[L - 1:L, :] for t in gam]

    kts = each(lambda t, c, d: stack(t * jnp.exp(c - d)), kap, cs, ld)
    rts = each(lambda t, u: stack(t * u), r, gam)
    khs = each(lambda t, u: stack(t * u), k2, ginv)
    bhs = each(lambda t, u: stack(t * u), beta, ginv)
    vs = each(stack, v)
    m = each(lambda t, u: _dot_nt(t, u) * msk_ref[0], kts, bhs)
    akk = each(lambda t, u: _dot_nt(t, u) * msk_ref[0], kts, khs)
    ark = each(lambda t, u: _dot_nt(t, u) * msk_ref[1], rts, khs)
    arb = each(lambda t, u: _dot_nt(t, u) * msk_ref[1], rts, bhs)

    md = each(lambda t: t * msk_ref[2], m)
    p2 = each(lambda t: _dot(t, t), md)
    p4 = each(lambda t: _dot(t, t), p2)
    p8 = each(lambda t: _dot(t, t), p4)
    xinv = each(lambda t: msk_ref[5] - t, md)
    for pw in (p2, p4, p8):
        xinv = each(lambda t, u: t + _dot(t, u), xinv, pw)
    for lvl in (3, 4):
        off = each(lambda t, u: _dot(t, u * msk_ref[lvl]), xinv, m)
        xinv = each(lambda t, u: t - _dot(u, t), xinv, off)

    ht = [ht_scr[n] for n in bs]
    wmat = each(lambda t, u, c, d: _dot_nt(t, u) + _dot(c, d), kts, ht, akk, vs)
    us = each(_dot, xinv, wmat)
    ys = each(lambda t, u, c, d, f, w: _dot_nt(t, u) + _dot(c, d) - _dot(f, w), rts, ht, ark, vs, arb, us)
    y = [t[0:L] + t[L:2 * L] + t[2 * L:3 * L] + t[3 * L:4 * L] for t in ys]

    lhs = each(lambda t, u: jnp.concatenate([t, u.astype(BF16)], axis=0), vs, us)
    rhs = each(lambda t, u, c, d: jnp.concatenate([stack(t * c * d), stack(-(u * c * d))], axis=0),
               k2, beta, ginv, g_last)
    hnew = each(lambda t, u, c, d: t * u + _dot_tn(c, d) * e32, ht, g_last, lhs, rhs)
    for n in bs:
        ht_scr[n] = hnew[n]

    d = each(lambda t: t - _dot_sel2(t, e) * (1.0 / HEAD_DIM), y)
    yv = each(lambda t: _dot_sel2(t * t, e) * (1.0 / HEAD_DIM), d)
    yn = each(lambda t, u: t * lax.rsqrt(u + GN_EPS) * lg_ref[...] + lb_ref[...], d, yv)
    bonus = each(lambda t, u, c: _dot_sel2(t * u * rk_ref[...], e) * c, r, k2, v)
    for n in bs:
        o_ref[n] = (yn[n] + bonus[n]) * g[n]


def _rwkv_consts(chunk):
    n = N_HEADS * chunk
    idx = jnp.arange(n)
    rh, rt = idx // chunk, idx % chunk
    same = rh[:, None] == rh[None, :]
    blk = lambda s: same & ((rt[:, None] // s) == (rt[None, :] // s))
    masks = jnp.stack([
        same & (rt[:, None] > rt[None, :]),
        same & (rt[:, None] >= rt[None, :]),
        blk(16),
        blk(32) & ~blk(16),
        blk(64) & ~blk(32),
        jnp.eye(n, dtype=bool),
    ]).astype(F32)
    lane_head = jnp.arange(BW) // HEAD_DIM
    e = (lane_head[:, None] == lane_head[None, :]).astype(BF16)
    hm = (jnp.arange(N_HEADS)[:, None] == lane_head[None, :]).astype(F32).reshape(N_HEADS, 1, BW)
    tri = (jnp.arange(chunk)[:, None] >= jnp.arange(chunk)[None, :]).astype(BF16)
    return e, tri, masks, hm


def _hi_lo(w):
    hi = w.astype(BF16)
    return jnp.stack([hi, (w - hi.astype(F32)).astype(BF16)])


def _rwkv(rw, params, consts, bsz, seq, chunk, nb):
    nc = seq // chunk
    n = N_HEADS * chunk
    mu, w0, w2p, a0, a2p, g2, k_k, k_a, r_k, lg, lb = params
    e, tri, masks, hm = consts
    c2 = lambda shape: pl.BlockSpec(shape, lambda b, c: (0, 0))
    c3 = lambda shape: pl.BlockSpec(shape, lambda b, c: (0, 0, 0))
    out = pl.pallas_call(
        functools.partial(_rwkv_kernel, chunk=chunk, nb=nb),
        grid=(bsz // nb, nc),
        in_specs=[pl.BlockSpec((nb, chunk, RWKV_PROJ), lambda b, c: (b, c, 0)),
                  c2((1, RWKV_PROJ)), c2((1, BW)), c3((2, LANE, BW)), c2((1, BW)), c3((2, LANE, BW)),
                  c3((2, LANE, BW)), c2((1, BW)), c2((1, BW)), c2((1, BW)), c2((1, BW)), c2((1, BW)),
                  c2((BW, BW)), c2((chunk, chunk)), c3((6, n, n)), c3((N_HEADS, 1, BW))],
        out_specs=pl.BlockSpec((nb, chunk, BW), lambda b, c: (b, c, 0)),
        out_shape=jax.ShapeDtypeStruct((bsz, seq, BW), F32),
        scratch_shapes=[pltpu.VMEM((nb, BW, BW), F32), pltpu.VMEM((nb, 8, RWKV_PROJ), F32)],
        compiler_params=_cp(("parallel", "arbitrary")),
        name="rwkv7_mix",
    )(rw.reshape(bsz, seq, RWKV_PROJ), mu, w0, _hi_lo(w2p), a0, _hi_lo(a2p), _hi_lo(g2),
      k_k, k_a, r_k, lg, lb, e, tri, masks, hm)
    return out.reshape(bsz * seq, BW)


def _merge_kernel(h_ref, y0_ref, y1_ref, y2_ref, y3_ref, wg_ref, wb_ref, wo_ref, g_ref, b_ref, o_ref):
    h = h_ref[...]
    hb = h.astype(BF16)
    merged = None
    for n, y_ref in enumerate((y0_ref, y1_ref, y2_ref, y3_ref)):
        gate = jax.nn.sigmoid(jnp.dot(hb, wg_ref[:, n * D_MODEL:(n + 1) * D_MODEL],
                                      preferred_element_type=F32))
        up = jnp.dot(y_ref[...].astype(BF16), wb_ref[n], preferred_element_type=F32)
        merged = gate * up if merged is None else merged + gate * up
    z = ALPHA * h + jnp.dot(merged.astype(BF16), wo_ref[...], preferred_element_type=F32)
    o_ref[...] = _layer_norm(z, g_ref[...], b_ref[...])


def _merge(h, y_s5, y_fox, y_rwkv, y_sb, wg, wb, wo, g, b, bsz, seq, tm):
    ns = seq // tm
    t = bsz * seq
    row = lambda bi, i: (bi * ns + i, 0)
    return pl.pallas_call(
        _merge_kernel,
        grid=(bsz, ns),
        in_specs=[pl.BlockSpec((tm, D_MODEL), row),
                  pl.BlockSpec((tm, BW), lambda bi, i: (i, bi)),
                  pl.BlockSpec((tm, BW), row), pl.BlockSpec((tm, BW), row), pl.BlockSpec((tm, BW), row),
                  pl.BlockSpec((D_MODEL, N_BRANCHES * D_MODEL), lambda bi, i: (0, 0)),
                  pl.BlockSpec((N_BRANCHES, BW, D_MODEL), lambda bi, i: (0, 0, 0)),
                  pl.BlockSpec((D_MODEL, D_MODEL), lambda bi, i: (0, 0)),
                  pl.BlockSpec((1, D_MODEL), lambda bi, i: (0, 0)),
                  pl.BlockSpec((1, D_MODEL), lambda bi, i: (0, 0))],
        out_specs=pl.BlockSpec((tm, D_MODEL), row),
        out_shape=jax.ShapeDtypeStruct((t, D_MODEL), F32),
        compiler_params=_cp(("parallel", "parallel")),
        name="merge_ln1",
    )(h, y_s5, y_fox, y_rwkv, y_sb, wg, wb, wo, g, b)


def _mlp_kernel(h_ref, p_ref, w1_ref, w2_ref, pw_ref, pg_ref, g_ref, b_ref, o_ref, *, ff_chunk):
    h = h_ref[...]
    hb = h.astype(BF16)
    ffn = None
    for c in range(D_FF // ff_chunk):
        sl = slice(c * ff_chunk, (c + 1) * ff_chunk)
        hid = jnp.maximum(jnp.dot(hb, w1_ref[:, sl], preferred_element_type=F32), 0.0)
        part = jnp.dot((hid * hid).astype(BF16), w2_ref[sl, :], preferred_element_type=F32)
        ffn = part if ffn is None else ffn + part
    ple = (jax.nn.sigmoid(jnp.dot(hb, pg_ref[...], preferred_element_type=F32))
           * jnp.dot(p_ref[...].astype(BF16), pw_ref[...], preferred_element_type=F32))
    o_ref[...] = _layer_norm(ALPHA * h + ffn + ple, g_ref[...], b_ref[...])


def _mlp(h, p, w1, w2, pw, pg, g, b, tm):
    t = h.shape[0]
    full = lambda shape: pl.BlockSpec(shape, lambda i: (0, 0))
    return pl.pallas_call(
        functools.partial(_mlp_kernel, ff_chunk=1024),
        grid=(t // tm,),
        in_specs=[pl.BlockSpec((tm, D_MODEL), lambda i: (i, 0)),
                  pl.BlockSpec((tm, PLE_DIM), lambda i: (i, 0)),
                  full((D_MODEL, D_FF)), full((D_FF, D_MODEL)), full((PLE_DIM, D_MODEL)),
                  full((D_MODEL, D_MODEL)), full((1, D_MODEL)), full((1, D_MODEL))],
        out_specs=pl.BlockSpec((tm, D_MODEL), lambda i: (i, 0)),
        out_shape=jax.ShapeDtypeStruct((t, D_MODEL), F32),
        compiler_params=_cp(("parallel",)),
        name="mlp_ple_ln2",
    )(h, p, w1, w2, pw, pg, g, b)


def _pad_rows(m, rows, at):
    out = jnp.zeros((rows, m.shape[1]), m.dtype)
    return out.at[at:at + m.shape[0]].set(m)


def kernel(x, p, w_in, s5_lambda_re, s5_lambda_im, s5_log_dt, s5_b_re, s5_b_im, s5_c_re, s5_c_im, s5_d, s5_glu_w, s5_glu_b, fox_f_bias, rwkv_mu, rwkv_w0, rwkv_w2, rwkv_a0, rwkv_a2, rwkv_g2, rwkv_k_k, rwkv_k_a, rwkv_r_k, rwkv_lnx_g, rwkv_lnx_b, w_branch, w_out, ln1_g, ln1_b, mlp_w1, mlp_w2, ple_w, ple_gate_w, ln2_g, ln2_b):
    bsz, seq, _ = x.shape
    t = bsz * seq
    tm = min(512, seq)
    ts = min(128, seq)
    chunk = RWKV_CHUNK
    scale = HEAD_DIM ** -0.5 * LOG2E

    rwkv_consts = _rwkv_consts(chunk)
    tri_lane = (jnp.arange(LANE)[:, None] <= jnp.arange(LANE)[None, :]).astype(F32)
    later_key = (jnp.arange(ATT_TILE)[:, None] < jnp.arange(ATT_TILE)[None, :]).astype(BF16)
    head_rows = (jnp.arange(8)[:, None] == jnp.arange(BW)[None, :] // HEAD_DIM).astype(BF16)
    row2 = lambda a: a.reshape(1, -1)

    h = x.reshape(t, D_MODEL)
    for i in range(DEPTH):
        w = w_in[i]
        w_main = jnp.concatenate([
            w[:, 0:256], w[:, 256:512] * scale, w[:, 512:768],
            w[:, 1028:2052],
            w[:, 2052:2308] * scale, w[:, 2308:2564]], axis=1).astype(BF16)
        w_vt = jnp.concatenate([w[:, 768:1024].T, w[:, 2564:2820].T], axis=0).astype(BF16)
        w_ff = _pad_rows(w[:, 1024:1028].T, 8, 0).astype(BF16)
        w_gates = w[:, 2820:].astype(BF16)

        u_sb, fqk, rw, sqk, fvt, svt, fft = _proj(h, w_main, w_vt, w_ff, bsz, seq, tm)

        bbar, cmat, lam = _s5_params(s5_lambda_re[i], s5_lambda_im[i], s5_log_dt[i], s5_b_re[i],
                                     s5_b_im[i], s5_c_re[i], s5_c_im[i], bsz)
        y_s5 = _s5(u_sb.reshape(seq * bsz, BW), bbar, cmat, lam, row2(s5_d[i]),
                   s5_glu_w[i].astype(BF16), row2(s5_glu_b[i]), bsz, seq, ts)
        y_s5 = y_s5.reshape(seq, bsz * BW)

        bias8 = _pad_rows(fox_f_bias[i].reshape(N_HEADS, 1), 8, 0)
        c_row = _fox_c(fft, bias8, tri_lane)
        ccol = jnp.transpose(c_row[:, :N_HEADS], (0, 2, 1)).reshape(t, N_HEADS)
        y_fox = _fox_attn(fqk, fvt, ccol, c_row, head_rows, bsz, seq)

        rparams = (row2(rwkv_mu[i]), row2(rwkv_w0[i]), _pad_rows(rwkv_w2[i], LANE, 0),
                   row2(rwkv_a0[i]), _pad_rows(rwkv_a2[i], LANE, 64), rwkv_g2[i],
                   row2(rwkv_k_k[i]), row2(rwkv_k_a[i]), row2(rwkv_r_k[i]),
                   row2(rwkv_lnx_g[i]), row2(rwkv_lnx_b[i]))
        y_rwkv = _rwkv(rw, rparams, rwkv_consts, bsz, seq, chunk, 8 if bsz % 8 == 0 else 1)

        y_sb = _sb_attn(sqk, svt, later_key, bsz, seq)

        h = _merge(h, y_s5, y_fox, y_rwkv, y_sb, w_gates, w_branch[i].astype(BF16),
                   w_out[i].astype(BF16), row2(ln1_g[i]), row2(ln1_b[i]), bsz, seq, tm)
        h = _mlp(h, p[i].reshape(t, PLE_DIM), mlp_w1[i].astype(BF16), mlp_w2[i].astype(BF16),
                 ple_w[i].astype(BF16), ple_gate_w[i].astype(BF16), row2(ln2_g[i]), row2(ln2_b[i]), tm)
    return h.reshape(bsz, seq, D_MODEL)
```

```python
import functools
import math

import jax
import jax.numpy as jnp
from jax import lax
from jax.experimental import pallas as pl
from jax.experimental.pallas import tpu as pltpu

F32 = jnp.float32
BF16 = jnp.bfloat16
HI = lax.Precision.HIGHEST

D_MODEL = 1024
PLE_DIM = 256
N_BRANCHES = 4
BW = 256
HEAD_DIM = 64
N_HEADS = 4
S5_GROUP_CH = 16
S5_GROUPS = 16
S5_STATE = 64
S5_N = S5_GROUPS * S5_STATE
RWKV_PROJ = 1024
D_FF = 4096
LN_EPS = 1e-5
GN_EPS = 64e-5
DEPTH = 2
ALPHA = (2 * DEPTH) ** 0.25
LOG2E = math.log2(math.e)
LANE = 128
RWKV_CHUNK = 64
ATT_TILE = 256
SB_DEAD = 160.0
FOX_DEAD = -152.0
VMEM_LIMIT = 56 * 1024 * 1024


def _cp(sem, vmem=VMEM_LIMIT):
    return pltpu.CompilerParams(dimension_semantics=sem, vmem_limit_bytes=vmem)


def _dot(a, b):
    return jnp.dot(a.astype(BF16), b.astype(BF16), preferred_element_type=F32)


def _dot_hi(a, b):
    return jnp.dot(a, b, precision=HI, preferred_element_type=F32)


def _dot_nt(a, b):
    return lax.dot_general(a.astype(BF16), b.astype(BF16), (((1,), (1,)), ((), ())),
                           preferred_element_type=F32)


def _dot_tn(a, b):
    return lax.dot_general(a.astype(BF16), b.astype(BF16), (((0,), (0,)), ((), ())),
                           preferred_element_type=F32)


def _softplus(x):
    return jnp.maximum(x, 0.0) + jnp.log1p(jnp.exp(-jnp.abs(x)))


def _gelu_tanh(x):
    return 0.5 * x * (1.0 + jnp.tanh(math.sqrt(2.0 / math.pi) * (x + 0.044715 * (x * x * x))))


def _layer_norm(z, g, b):
    mu = jnp.mean(z, axis=-1, keepdims=True)
    d = z - mu
    var = jnp.mean(d * d, axis=-1, keepdims=True)
    return d * lax.rsqrt(var + LN_EPS) * g + b


def _proj_kernel(h_ref, wm_ref, wvt_ref, wf_ref, u_ref, fqk_ref, rw_ref, sqk_ref, fvt_ref, svt_ref,
                 fft_ref, *, tm):
    hb = h_ref[...].astype(BF16)
    u_ref[...] = jnp.dot(hb, wm_ref[:, 0:256], preferred_element_type=F32)
    fqk_ref[...] = jnp.dot(hb, wm_ref[:, 256:768], preferred_element_type=F32).astype(BF16)
    rw_ref[...] = jnp.dot(hb, wm_ref[:, 768:1792], preferred_element_type=F32)
    sqk_ref[...] = jnp.dot(hb, wm_ref[:, 1792:2304], preferred_element_type=F32).astype(BF16)
    vt = _dot_nt(wvt_ref[...], hb).astype(BF16)
    for jj in range(tm // ATT_TILE):
        sl = slice(jj * ATT_TILE, (jj + 1) * ATT_TILE)
        fvt_ref[0, jj] = vt[0:BW, sl]
        svt_ref[0, jj] = vt[BW:2 * BW, sl]
    fft_ref[0] = _dot_nt(wf_ref[...], hb)


def _proj(h, wm, wvt, wf, bsz, seq, tm):
    ns = seq // tm
    t = bsz * seq
    row = lambda b, i: (b * ns + i, 0)
    vt_spec = pl.BlockSpec((1, tm // ATT_TILE, BW, ATT_TILE), lambda b, i: (b, i, 0, 0))
    vt_shape = jax.ShapeDtypeStruct((bsz, seq // ATT_TILE, BW, ATT_TILE), BF16)
    return pl.pallas_call(
        functools.partial(_proj_kernel, tm=tm),
        grid=(bsz, ns),
        in_specs=[pl.BlockSpec((tm, D_MODEL), row),
                  pl.BlockSpec((D_MODEL, 2304), lambda b, i: (0, 0)),
                  pl.BlockSpec((2 * BW, D_MODEL), lambda b, i: (0, 0)),
                  pl.BlockSpec((8, D_MODEL), lambda b, i: (0, 0))],
        out_specs=[pl.BlockSpec((tm, BW), lambda b, i: (i, b)),
                   pl.BlockSpec((tm, 2 * BW), row),
                   pl.BlockSpec((tm, RWKV_PROJ), row),
                   pl.BlockSpec((tm, 2 * BW), row),
                   vt_spec, vt_spec,
                   pl.BlockSpec((1, 8, tm), lambda b, i: (b, 0, i))],
        out_shape=[jax.ShapeDtypeStruct((seq, bsz * BW), F32),
                   jax.ShapeDtypeStruct((t, 2 * BW), BF16),
                   jax.ShapeDtypeStruct((t, RWKV_PROJ), F32),
                   jax.ShapeDtypeStruct((t, 2 * BW), BF16),
                   vt_shape, vt_shape,
                   jax.ShapeDtypeStruct((bsz, 8, seq), F32)],
        compiler_params=_cp(("parallel", "parallel")),
        name="in_proj",
    )(h, wm, wvt, wf)


def _s5_kernel(u_ref, bbar_ref, cmat_ref, lam_ref, d_ref, gw_ref, gb_ref, y_ref, xs_scr, st_scr,
               *, ts, bsz):
    @pl.when(pl.program_id(0) == 0)
    def _():
        st_scr[...] = jnp.zeros(st_scr.shape, F32)

    u = u_ref[...]
    ub = u.astype(BF16)
    half = (ts * bsz) // 2
    xs_scr[0:half, :] = jnp.dot(ub[0:half], bbar_ref[...], preferred_element_type=F32)
    xs_scr[half:2 * half, :] = jnp.dot(ub[half:2 * half], bbar_ref[...], preferred_element_type=F32)
    lr = lam_ref[:, 0:S5_N]
    li = lam_ref[:, S5_N:2 * S5_N]

    def body(t, carry):
        xr, xi = carry
        r0 = pl.multiple_of(t * bsz, bsz)
        br = xs_scr[pl.ds(r0, bsz), 0:S5_N]
        bi = xs_scr[pl.ds(r0, bsz), S5_N:2 * S5_N]
        nr = lr * xr - li * xi + br
        ni = lr * xi + li * xr + bi
        xs_scr[pl.ds(r0, bsz), 0:S5_N] = nr
        xs_scr[pl.ds(r0, bsz), S5_N:2 * S5_N] = ni
        return nr, ni

    xr, xi = lax.fori_loop(0, ts, body, (st_scr[:, 0:S5_N], st_scr[:, S5_N:2 * S5_N]), unroll=4)
    st_scr[:, 0:S5_N] = xr
    st_scr[:, S5_N:2 * S5_N] = xi

    y = jnp.concatenate(
        [jnp.dot(xs_scr[0:half, :].astype(BF16), cmat_ref[...], preferred_element_type=F32),
         jnp.dot(xs_scr[half:2 * half, :].astype(BF16), cmat_ref[...], preferred_element_type=F32)],
        axis=0) + d_ref[...] * u
    y = _gelu_tanh(y)
    y_ref[...] = y * jax.nn.sigmoid(_dot(y, gw_ref[...]) + gb_ref[...])


def _s5(u_sb, bbar, cmat, lam, dvec, gw, gb, bsz, seq, ts):
    rows = ts * bsz
    full = lambda shape: pl.BlockSpec(shape, lambda i: (0, 0))
    return pl.pallas_call(
        functools.partial(_s5_kernel, ts=ts, bsz=bsz),
        grid=(seq // ts,),
        in_specs=[pl.BlockSpec((rows, BW), lambda i: (i, 0)),
                  full((BW, 2 * S5_N)), full((2 * S5_N, BW)), full((bsz, 2 * S5_N)),
                  full((1, BW)), full((BW, BW)), full((1, BW))],
        out_specs=pl.BlockSpec((rows, BW), lambda i: (i, 0)),
        out_shape=jax.ShapeDtypeStruct((seq * bsz, BW), F32),
        scratch_shapes=[pltpu.VMEM((rows, 2 * S5_N), F32), pltpu.VMEM((bsz, 2 * S5_N), F32)],
        compiler_params=_cp(("arbitrary",)),
        name="s5_mixer",
    )(u_sb, bbar, cmat, lam, dvec, gw, gb)


def _s5_params(lam_re, lam_im, log_dt, b_re, b_im, c_re, c_im, bsz):
    dt = jnp.exp(log_dt)[:, None]
    mag = jnp.exp(lam_re * dt)
    ang = lam_im * dt
    lb_re = mag * jnp.cos(ang)
    lb_im = mag * jnp.sin(ang)
    den = jnp.square(lam_re) + jnp.square(lam_im)
    nr = lb_re - 1.0
    f_re = (nr * lam_re + lb_im * lam_im) / den
    f_im = (lb_im * lam_re - nr * lam_im) / den
    bb_re = f_re[..., None] * b_re - f_im[..., None] * b_im
    bb_im = f_re[..., None] * b_im + f_im[..., None] * b_re
    eye = jnp.eye(S5_GROUPS, dtype=F32)
    bd_in = lambda m: jnp.einsum('gph,gk->ghkp', m, eye).reshape(BW, S5_N)
    bd_out = lambda m: jnp.einsum('ghp,gk->gpkh', m, eye).reshape(S5_N, BW)
    bbar = jnp.concatenate([bd_in(bb_re), bd_in(bb_im)], axis=1).astype(BF16)
    cmat = jnp.concatenate([bd_out(c_re), -bd_out(c_im)], axis=0).astype(BF16)
    lam = jnp.concatenate([lb_re.reshape(1, S5_N), lb_im.reshape(1, S5_N)], axis=1)
    return bbar, cmat, jnp.broadcast_to(lam, (bsz, 2 * S5_N))


def _foxc_kernel(ff_ref, bias_ref, tri_ref, c_ref, *, n_chunks):
    carry = jnp.zeros((8, 1), F32)
    for ch in range(n_chunks):
        sl = slice(ch * LANE, (ch + 1) * LANE)
        x = ff_ref[0, :, sl] + bias_ref[...]
        lf = jnp.minimum(x, 0.0) - jnp.log1p(jnp.exp(-jnp.abs(x)))
        cs = _dot_hi(lf, tri_ref[...]) + carry
        c_ref[0, :, sl] = cs * LOG2E
        carry = cs[:, LANE - 1:LANE]


def _fox_c(fft, bias8, tri):
    bsz, _, seq = fft.shape
    return pl.pallas_call(
        functools.partial(_foxc_kernel, n_chunks=seq // LANE),
        grid=(bsz,),
        in_specs=[pl.BlockSpec((1, 8, seq), lambda b: (b, 0, 0)),
                  pl.BlockSpec((8, 1), lambda b: (0, 0)),
                  pl.BlockSpec((LANE, LANE), lambda b: (0, 0))],
        out_specs=pl.BlockSpec((1, 8, seq), lambda b: (b, 0, 0)),
        out_shape=jax.ShapeDtypeStruct((bsz, 8, seq), F32),
        compiler_params=_cp(("parallel",)),
        name="fox_cumlogf",
    )(fft, bias8, tri)


def _head_select(x, hh):
    lane = lax.broadcasted_iota(jnp.int32, x.shape, 1)
    sel = (lane < HEAD_DIM) if hh == 0 else (lane >= HEAD_DIM)
    return jnp.where(sel, x, jnp.zeros_like(x))


def _pair_cols(h):
    return slice((h // 2) * LANE, (h // 2 + 1) * LANE)


def _head_rows(h):
    return slice(h * HEAD_DIM, (h + 1) * HEAD_DIM)


def _split_heads(q):
    return [_head_select(q[:, _pair_cols(h)], h % 2) for h in range(N_HEADS)]


def _tile_positions():
    kpos = lax.broadcasted_iota(jnp.int32, (ATT_TILE, ATT_TILE), 0)
    qpos = lax.broadcasted_iota(jnp.int32, (ATT_TILE, ATT_TILE), 1)
    return kpos, qpos


def _scores(k_ref, j, qh):
    ks = k_ref[pl.ds(pl.multiple_of(j * ATT_TILE, ATT_TILE), ATT_TILE), :]
    return tuple(_dot_nt(ks[:, _pair_cols(h)], qh[h]) for h in range(N_HEADS))

def _head_sq_norms(esel, x):
    xf = x.astype(F32)
    hi, lo = _split2(xf * xf)
    return _dot_nt(esel, hi) + _dot_nt(esel, lo)


def _fox_kernel(q_ref, k_ref, vt_ref, ck_ref, cq_ref, esel_ref, o_ref, acc_scr, st_scr, s_scr, kn_scr,
                *, n_tiles):
    i = pl.program_id(1)
    q = q_ref[...]
    qh = _split_heads(q)
    cq = [cq_ref[0, h:h + 1, :] for h in range(N_HEADS)]
    esel = esel_ref[...]
    hs = range(N_HEADS)
    acc_scr[...] = jnp.zeros(acc_scr.shape, F32)
    for h in hs:
        st_scr[2 * h:2 * h + 1, :] = jnp.full((1, ATT_TILE), -1e30, F32)
        st_scr[2 * h + 1:2 * h + 2, :] = jnp.zeros((1, ATT_TILE), F32)

    @pl.when(i == 0)
    def _():
        def widen(j, mx):
            ks = k_ref[pl.ds(pl.multiple_of(j * ATT_TILE, ATT_TILE), ATT_TILE), :]
            return jnp.maximum(mx, _head_sq_norms(esel, ks))
        mx = lax.fori_loop(0, n_tiles, widen, jnp.zeros((8, ATT_TILE), F32))
        kn_scr[...] = jnp.broadcast_to(jnp.max(mx, axis=1, keepdims=True), kn_scr.shape)

    qn = _head_sq_norms(esel, q)
    reach = [jnp.sqrt(qn[h:h + 1, :] * kn_scr[h:h + 1, 0:1]) * 1.001 + 1.0 + cq[h] for h in hs]

    def tile(j, s, masked):
        k0 = pl.multiple_of(j * ATT_TILE, ATT_TILE)
        vt = vt_ref[0, j]
        m_old = [st_scr[2 * h:2 * h + 1, :] for h in hs]
        l_old = [st_scr[2 * h + 1:2 * h + 2, :] for h in hs]
        t1 = [s[h] - ck_ref[pl.ds(k0, ATT_TILE), h:h + 1] for h in hs]
        if masked:
            kpos, qpos = _tile_positions()
            t1 = [jnp.where(kpos <= qpos, t, -1e30) for t in t1]
        m_new = [jnp.maximum(m_old[h], jnp.max(t1[h], axis=0, keepdims=True) + cq[h]) for h in hs]
        alpha = [jnp.exp2(m_old[h] - m_new[h]) for h in hs]
        p = [jnp.exp2(t1[h] - (m_new[h] - cq[h])) for h in hs]
        l = [alpha[h] * l_old[h] + jnp.sum(p[h], axis=0, keepdims=True) for h in hs]
        pv = [jnp.dot(vt[_head_rows(h), :], p[h].astype(BF16), preferred_element_type=F32) for h in hs]
        for h in hs:
            acc_scr[h] = alpha[h] * acc_scr[h] + pv[h]
            st_scr[2 * h:2 * h + 1, :] = m_new[h]
            st_scr[2 * h + 1:2 * h + 2, :] = l[h]

    def held():
        return tuple(s_scr[h] for h in hs)

    def hold(s):
        for h in hs:
            s_scr[h] = s[h]

    def all_dead(newest_unvisited):
        row = jnp.maximum(newest_unvisited, 0)
        worst = None
        for h in hs:
            gap = reach[h] - ck_ref[pl.ds(row, 1), h:h + 1] - st_scr[2 * h:2 * h + 1, :]
            worst = gap if worst is None else jnp.maximum(worst, gap)
        return jnp.max(worst) < FOX_DEAD

    def pair(carry):
        jj, _ = carry
        first = i - 1 - 2 * jj
        rest_dead = all_dead((first - 1) * ATT_TILE - 1)
        s_first = held()
        s_second = _scores(k_ref, first - 1, qh)
        tile(first, s_first, False)
        hold(_scores(k_ref, jnp.maximum(first - 2, 0), qh))
        tile(first - 1, s_second, False)
        return jj + 1, rest_dead

    s_diag = _scores(k_ref, i, qh)
    hold(_scores(k_ref, jnp.maximum(i - 1, 0), qh))
    tile(i, s_diag, True)
    _, dead = lax.while_loop(lambda c: jnp.logical_and(c[0] < i // 2, jnp.logical_not(c[1])),
                             pair, (jnp.int32(0), all_dead(i * ATT_TILE - 1)))

    @pl.when(jnp.logical_and(lax.rem(i, 2) == 1, jnp.logical_not(dead)))
    def _():
        tile(0, held(), False)

    out_t = jnp.concatenate([acc_scr[h] / st_scr[2 * h + 1:2 * h + 2, :] for h in hs], axis=0)
    o_ref[...] = out_t.T


def _att_specs(seq, nq):
    return [pl.BlockSpec((ATT_TILE, BW), lambda b, i: (b * nq + i, 0)),
            pl.BlockSpec((seq, BW), lambda b, i: (b, 1)),
            pl.BlockSpec((1, seq // ATT_TILE, BW, ATT_TILE), lambda b, i: (b, 0, 0, 0))]


def _fox_attn(qk, vt, ccol, crow, esel, bsz, seq):
    nq = seq // ATT_TILE
    return pl.pallas_call(
        functools.partial(_fox_kernel, n_tiles=nq),
        grid=(bsz, nq),
        in_specs=_att_specs(seq, nq) + [
            pl.BlockSpec((seq, N_HEADS), lambda b, i: (b, 0)),
            pl.BlockSpec((1, 8, ATT_TILE), lambda b, i: (b, 0, i)),
            pl.BlockSpec((8, BW), lambda b, i: (0, 0))],
        out_specs=pl.BlockSpec((ATT_TILE, BW), lambda b, i: (b * nq + i, 0)),
        out_shape=jax.ShapeDtypeStruct((bsz * seq, BW), F32),
        scratch_shapes=[pltpu.VMEM((N_HEADS, HEAD_DIM, ATT_TILE), F32),
                        pltpu.VMEM((2 * N_HEADS, ATT_TILE), F32),
                        pltpu.VMEM((N_HEADS, ATT_TILE, ATT_TILE), F32),
                        pltpu.VMEM((8, LANE), F32)],
        compiler_params=_cp(("parallel", "arbitrary")),
        name="fox_attention",
    )(qk, qk, vt, ccol, crow, esel)


def _sb_kernel(q_ref, k_ref, vt_ref, w_ref, o_ref, acc_scr, st_scr, s_scr):
    i = pl.program_id(1)
    qh = _split_heads(q_ref[...])
    wmat = w_ref[...]
    hs = range(N_HEADS)
    acc_scr[...] = jnp.zeros(acc_scr.shape, F32)
    st_scr[...] = jnp.zeros(st_scr.shape, F32)

    def tile(j, z, masked):
        vt = vt_ref[0, j]
        seen = [st_scr[h:h + 1, :] for h in hs]
        sp = [jnp.maximum(t, 0.0) + jnp.log(1.0 + jnp.exp2(-jnp.abs(t))) * LOG2E for t in z]
        if masked:
            kpos, qpos = _tile_positions()
            mask = kpos < qpos
            sp = [jnp.where(mask, t, 0.0) for t in sp]
        later = [jnp.dot(wmat, t.astype(BF16), preferred_element_type=F32) for t in sp]
        w = [jnp.exp2((z[h] - sp[h]) - later[h] - seen[h]) for h in hs]
        if masked:
            w = [jnp.where(mask, t, 0.0) for t in w]
        pv = [jnp.dot(vt[_head_rows(h), :], w[h].astype(BF16), preferred_element_type=F32) for h in hs]
        for h in hs:
            acc_scr[h] = acc_scr[h] + pv[h]
            st_scr[h:h + 1, :] = seen[h] + jnp.sum(sp[h], axis=0, keepdims=True)

    def held():
        return tuple(s_scr[h] for h in hs)

    def hold(z):
        for h in hs:
            s_scr[h] = z[h]

    def all_dead():
        return jnp.min(st_scr[0:N_HEADS, :]) >= SB_DEAD

    def pair(carry):
        jj, _ = carry
        first = i - 2 - 2 * jj
        z_first = held()
        z_second = _scores(k_ref, first - 1, qh)
        tile(first, z_first, False)
        hold(_scores(k_ref, jnp.maximum(first - 2, 0), qh))
        tile(first - 1, z_second, False)
        return jj + 1, all_dead()

    z_diag = _scores(k_ref, i, qh)
    hold(_scores(k_ref, jnp.maximum(i - 1, 0), qh))
    tile(i, z_diag, True)

    @pl.when(i >= 1)
    def _():
        z_prev = held()
        hold(_scores(k_ref, jnp.maximum(i - 2, 0), qh))
        tile(i - 1, z_prev, False)

    rest = jnp.maximum(i - 1, 0)
    _, dead = lax.while_loop(lambda c: jnp.logical_and(c[0] < rest // 2, jnp.logical_not(c[1])),
                             pair, (jnp.int32(0), all_dead()))

    @pl.when(jnp.logical_and(lax.rem(rest, 2) == 1, jnp.logical_not(dead)))
    def _():
        tile(0, held(), False)

    o_ref[...] = jnp.concatenate([acc_scr[h] for h in hs], axis=0).T


def _sb_attn(qk, vt, wmat, bsz, seq):
    nq = seq // ATT_TILE
    return pl.pallas_call(
        _sb_kernel,
        grid=(bsz, nq),
        in_specs=_att_specs(seq, nq) + [pl.BlockSpec((ATT_TILE, ATT_TILE), lambda b, i: (0, 0))],
        out_specs=pl.BlockSpec((ATT_TILE, BW), lambda b, i: (b * nq + i, 0)),
        out_shape=jax.ShapeDtypeStruct((bsz * seq, BW), F32),
        scratch_shapes=[pltpu.VMEM((N_HEADS, HEAD_DIM, ATT_TILE), F32),
                        pltpu.VMEM((8, ATT_TILE), F32),
                        pltpu.VMEM((N_HEADS, ATT_TILE, ATT_TILE), F32)],
        compiler_params=_cp(("parallel", "parallel")),
        name="sb_attention",
    )(qk, qk, vt, wmat)


def _split2(x):
    hi = x.astype(BF16)
    return hi, (x - hi.astype(F32)).astype(BF16)


def _dot_w3(x, w_ref):
    hi, lo = _split2(x)
    return (jnp.dot(hi, w_ref[0], preferred_element_type=F32)
            + jnp.dot(lo, w_ref[0], preferred_element_type=F32)
            + jnp.dot(hi, w_ref[1], preferred_element_type=F32))


def _dot_sel2(x, sel):
    hi, lo = _split2(x)
    return jnp.dot(hi, sel, preferred_element_type=F32) + jnp.dot(lo, sel, preferred_element_type=F32)


def _dot_sel3_left(sel, x):
    hi = x.astype(BF16)
    r1 = x - hi.astype(F32)
    mid = r1.astype(BF16)
    lo = (r1 - mid.astype(F32)).astype(BF16)
    return (jnp.dot(sel, hi, preferred_element_type=F32) + jnp.dot(sel, mid, preferred_element_type=F32)
            + jnp.dot(sel, lo, preferred_element_type=F32))


def _rwkv_kernel(rw_ref, mu_ref, w0_ref, w2_ref, a0_ref, a2_ref, g2_ref, kk_ref, ka_ref, rk_ref,
                 lg_ref, lb_ref, e_ref, tri_ref, msk_ref, hm_ref, o_ref, ht_scr, last_scr,
                 *, chunk, nb):
    L = chunk

    @pl.when(pl.program_id(1) == 0)
    def _():
        ht_scr[...] = jnp.zeros(ht_scr.shape, F32)
        last_scr[...] = jnp.zeros(last_scr.shape, F32)

    e = e_ref[...]
    e32 = e.astype(F32)
    bs = range(nb)

    def stack(t):
        return jnp.concatenate([t * hm_ref[h] for h in range(N_HEADS)], axis=0).astype(BF16)

    def each(f, *lists):
        return [f(*args) for args in zip(*lists)]

    def shift_mix(n):
        x = rw_ref[n]
        row = lax.broadcasted_iota(jnp.int32, x.shape, 0)
        prev = jnp.where(row == 0, last_scr[n, 0:1, :], pltpu.roll(x, 1, axis=0))
        last_scr[n, 0:1, :] = x[L - 1:L, :]
        return x + (prev - x) * mu_ref[...]

    xs = each(shift_mix, bs)
    r = [t[:, 0:256] for t in xs]
    k = [t[:, 256:512] for t in xs]
    v = [t[:, 512:768] for t in xs]
    wa = [t[:, 768:896] for t in xs]
    g1 = [t[:, 896:1024] for t in xs]
    wpre = each(lambda t: w0_ref[...] + _dot_w3(jnp.tanh(t), w2_ref), wa)
    ld = each(lambda t: -jnp.exp(-_softplus(-t) - 0.5), wpre)
    a = each(lambda t: jax.nn.sigmoid(a0_ref[...] + _dot_w3(t, a2_ref)), wa)
    g = each(lambda t: _dot_w3(jax.nn.sigmoid(t), g2_ref), g1)
    kk = each(lambda t: t * kk_ref[...], k)
    kap = each(lambda t: t / jnp.maximum(jnp.sqrt(_dot_sel2(t * t, e)), 1e-12), kk)
    k2 = each(lambda t, u: t * (1.0 + (u - 1.0) * ka_ref[...]), k, a)
    beta = each(lambda t, u: t * u, kap, a)

    cs = each(lambda t: _dot_sel3_left(tri_ref[...], t), ld)
    gam = each(jnp.exp, cs)
    ginv = each(lambda t: jnp.exp(-t), cs)
    g_last = [t[L - 1:L, :] for t in gam]

    kts = each(lambda t, c, d: stack(t * jnp.exp(c - d)), kap, cs, ld)
    rts = each(lambda t, u: stack(t * u), r, gam)
    khs = each(lambda t, u: stack(t * u), k2, ginv)
    bhs = each(lambda t, u: stack(t * u), beta, ginv)
    vs = each(stack, v)
    m = each(lambda t, u: _dot_nt(t, u) * msk_ref[0], kts, bhs)
    akk = each(lambda t, u: _dot_nt(t, u) * msk_ref[0], kts, khs)
    ark = each(lambda t, u: _dot_nt(t, u) * msk_ref[1], rts, khs)
    arb = each(lambda t, u: _dot_nt(t, u) * msk_ref[1], rts, bhs)

    md = each(lambda t: t * msk_ref[2], m)
    p2 = each(lambda t: _dot(t, t), md)
    p4 = each(lambda t: _dot(t, t), p2)
    p8 = each(lambda t: _dot(t, t), p4)
    xinv = each(lambda t: msk_ref[5] - t, md)
    for pw in (p2, p4, p8):
        xinv = each(lambda t, u: t + _dot(t, u), xinv, pw)
    for lvl in (3, 4):
        off = each(lambda t, u: _dot(t, u * msk_ref[lvl]), xinv, m)
        xinv = each(lambda t, u: t - _dot(u, t), xinv, off)

    ht = [ht_scr[n] for n in bs]
    wmat = each(lambda t, u, c, d: _dot_nt(t, u) + _dot(c, d), kts, ht, akk, vs)
    us = each(_dot, xinv, wmat)
    ys = each(lambda t, u, c, d, f, w: _dot_nt(t, u) + _dot(c, d) - _dot(f, w), rts, ht, ark, vs, arb, us)
    y = [t[0:L] + t[L:2 * L] + t[2 * L:3 * L] + t[3 * L:4 * L] for t in ys]

    lhs = each(lambda t, u: jnp.concatenate([t, u.astype(BF16)], axis=0), vs, us)
    rhs = each(lambda t, u, c, d: jnp.concatenate([stack(t * c * d), stack(-(u * c * d))], axis=0),
               k2, beta, ginv, g_last)
    hnew = each(lambda t, u, c, d: t * u + _dot_tn(c, d) * e32, ht, g_last, lhs, rhs)
    for n in bs:
        ht_scr[n] = hnew[n]

    d = each(lambda t: t - _dot_sel2(t, e) * (1.0 / HEAD_DIM), y)
    yv = each(lambda t: _dot_sel2(t * t, e) * (1.0 / HEAD_DIM), d)
    yn = each(lambda t, u: t * lax.rsqrt(u + GN_EPS) * lg_ref[...] + lb_ref[...], d, yv)
    bonus = each(lambda t, u, c: _dot_sel2(t * u * rk_ref[...], e) * c, r, k2, v)
    for n in bs:
        o_ref[n] = (yn[n] + bonus[n]) * g[n]


def _rwkv_consts(chunk):
    n = N_HEADS * chunk
    idx = jnp.arange(n)
    rh, rt = idx // chunk, idx % chunk
    same = rh[:, None] == rh[None, :]
    blk = lambda s: same & ((rt[:, None] // s) == (rt[None, :] // s))
    masks = jnp.stack([
        same & (rt[:, None] > rt[None, :]),
        same & (rt[:, None] >= rt[None, :]),
        blk(16),
        blk(32) & ~blk(16),
        blk(64) & ~blk(32),
        jnp.eye(n, dtype=bool),
    ]).astype(F32)
    lane_head = jnp.arange(BW) // HEAD_DIM
    e = (lane_head[:, None] == lane_head[None, :]).astype(BF16)
    hm = (jnp.arange(N_HEADS)[:, None] == lane_head[None, :]).astype(F32).reshape(N_HEADS, 1, BW)
    tri = (jnp.arange(chunk)[:, None] >= jnp.arange(chunk)[None, :]).astype(BF16)
    return e, tri, masks, hm


def _hi_lo(w):
    hi = w.astype(BF16)
    return jnp.stack([hi, (w - hi.astype(F32)).astype(BF16)])


def _rwkv(rw, params, consts, bsz, seq, chunk, nb):
    nc = seq // chunk
    n = N_HEADS * chunk
    mu, w0, w2p, a0, a2p, g2, k_k, k_a, r_k, lg, lb = params
    e, tri, masks, hm = consts
    c2 = lambda shape: pl.BlockSpec(shape, lambda b, c: (0, 0))
    c3 = lambda shape: pl.BlockSpec(shape, lambda b, c: (0, 0, 0))
    out = pl.pallas_call(
        functools.partial(_rwkv_kernel, chunk=chunk, nb=nb),
        grid=(bsz // nb, nc),
        in_specs=[pl.BlockSpec((nb, chunk, RWKV_PROJ), lambda b, c: (b, c, 0)),
                  c2((1, RWKV_PROJ)), c2((1, BW)), c3((2, LANE, BW)), c2((1, BW)), c3((2, LANE, BW)),
                  c3((2, LANE, BW)), c2((1, BW)), c2((1, BW)), c2((1, BW)), c2((1, BW)), c2((1, BW)),
                  c2((BW, BW)), c2((chunk, chunk)), c3((6, n, n)), c3((N_HEADS, 1, BW))],
        out_specs=pl.BlockSpec((nb, chunk, BW), lambda b, c: (b, c, 0)),
        out_shape=jax.ShapeDtypeStruct((bsz, seq, BW), F32),
        scratch_shapes=[pltpu.VMEM((nb, BW, BW), F32), pltpu.VMEM((nb, 8, RWKV_PROJ), F32)],
        compiler_params=_cp(("parallel", "arbitrary")),
        name="rwkv7_mix",
    )(rw.reshape(bsz, seq, RWKV_PROJ), mu, w0, _hi_lo(w2p), a0, _hi_lo(a2p), _hi_lo(g2),
      k_k, k_a, r_k, lg, lb, e, tri, masks, hm)
    return out.reshape(bsz * seq, BW)


def _merge_kernel(h_ref, y0_ref, y1_ref, y2_ref, y3_ref, wg_ref, wb_ref, wo_ref, g_ref, b_ref, o_ref):
    h = h_ref[...]
    hb = h.astype(BF16)
    merged = None
    for n, y_ref in enumerate((y0_ref, y1_ref, y2_ref, y3_ref)):
        gate = jax.nn.sigmoid(jnp.dot(hb, wg_ref[:, n * D_MODEL:(n + 1) * D_MODEL],
                                      preferred_element_type=F32))
        up = jnp.dot(y_ref[...].astype(BF16), wb_ref[n], preferred_element_type=F32)
        merged = gate * up if merged is None else merged + gate * up
    z = ALPHA * h + jnp.dot(merged.astype(BF16), wo_ref[...], preferred_element_type=F32)
    o_ref[...] = _layer_norm(z, g_ref[...], b_ref[...])


def _merge(h, y_s5, y_fox, y_rwkv, y_sb, wg, wb, wo, g, b, bsz, seq, tm):
    ns = seq // tm
    t = bsz * seq
    row = lambda bi, i: (bi * ns + i, 0)
    return pl.pallas_call(
        _merge_kernel,
        grid=(bsz, ns),
        in_specs=[pl.BlockSpec((tm, D_MODEL), row),
                  pl.BlockSpec((tm, BW), lambda bi, i: (i, bi)),
                  pl.BlockSpec((tm, BW), row), pl.BlockSpec((tm, BW), row), pl.BlockSpec((tm, BW), row),
                  pl.BlockSpec((D_MODEL, N_BRANCHES * D_MODEL), lambda bi, i: (0, 0)),
                  pl.BlockSpec((N_BRANCHES, BW, D_MODEL), lambda bi, i: (0, 0, 0)),
                  pl.BlockSpec((D_MODEL, D_MODEL), lambda bi, i: (0, 0)),
                  pl.BlockSpec((1, D_MODEL), lambda bi, i: (0, 0)),
                  pl.BlockSpec((1, D_MODEL), lambda bi, i: (0, 0))],
        out_specs=pl.BlockSpec((tm, D_MODEL), row),
        out_shape=jax.ShapeDtypeStruct((t, D_MODEL), F32),
        compiler_params=_cp(("parallel", "parallel")),
        name="merge_ln1",
    )(h, y_s5, y_fox, y_rwkv, y_sb, wg, wb, wo, g, b)


def _mlp_kernel(h_ref, p_ref, w1_ref, w2_ref, pw_ref, pg_ref, g_ref, b_ref, o_ref, *, ff_chunk):
    h = h_ref[...]
    hb = h.astype(BF16)
    ffn = None
    for c in range(D_FF // ff_chunk):
        sl = slice(c * ff_chunk, (c + 1) * ff_chunk)
        hid = jnp.maximum(jnp.dot(hb, w1_ref[:, sl], preferred_element_type=F32), 0.0)
        part = jnp.dot((hid * hid).astype(BF16), w2_ref[sl, :], preferred_element_type=F32)
        ffn = part if ffn is None else ffn + part
    ple = (jax.nn.sigmoid(jnp.dot(hb, pg_ref[...], preferred_element_type=F32))
           * jnp.dot(p_ref[...].astype(BF16), pw_ref[...], preferred_element_type=F32))
    o_ref[...] = _layer_norm(ALPHA * h + ffn + ple, g_ref[...], b_ref[...])


def _mlp(h, p, w1, w2, pw, pg, g, b, tm):
    t = h.shape[0]
    full = lambda shape: pl.BlockSpec(shape, lambda i: (0, 0))
    return pl.pallas_call(
        functools.partial(_mlp_kernel, ff_chunk=1024),
        grid=(t // tm,),
        in_specs=[pl.BlockSpec((tm, D_MODEL), lambda i: (i, 0)),
                  pl.BlockSpec((tm, PLE_DIM), lambda i: (i, 0)),
                  full((D_MODEL, D_FF)), full((D_FF, D_MODEL)), full((PLE_DIM, D_MODEL)),
                  full((D_MODEL, D_MODEL)), full((1, D_MODEL)), full((1, D_MODEL))],
        out_specs=pl.BlockSpec((tm, D_MODEL), lambda i: (i, 0)),
        out_shape=jax.ShapeDtypeStruct((t, D_MODEL), F32),
        compiler_params=_cp(("parallel",)),
        name="mlp_ple_ln2",
    )(h, p, w1, w2, pw, pg, g, b)


def _pad_rows(m, rows, at):
    out = jnp.zeros((rows, m.shape[1]), m.dtype)
    return out.at[at:at + m.shape[0]].set(m)


def kernel(x, p, w_in, s5_lambda_re, s5_lambda_im, s5_log_dt, s5_b_re, s5_b_im, s5_c_re, s5_c_im, s5_d, s5_glu_w, s5_glu_b, fox_f_bias, rwkv_mu, rwkv_w0, rwkv_w2, rwkv_a0, rwkv_a2, rwkv_g2, rwkv_k_k, rwkv_k_a, rwkv_r_k, rwkv_lnx_g, rwkv_lnx_b, w_branch, w_out, ln1_g, ln1_b, mlp_w1, mlp_w2, ple_w, ple_gate_w, ln2_g, ln2_b):
    bsz, seq, _ = x.shape
    t = bsz * seq
    tm = min(512, seq)
    ts = min(128, seq)
    chunk = RWKV_CHUNK
    scale = HEAD_DIM ** -0.5 * LOG2E

    rwkv_consts = _rwkv_consts(chunk)
    tri_lane = (jnp.arange(LANE)[:, None] <= jnp.arange(LANE)[None, :]).astype(F32)
    later_key = (jnp.arange(ATT_TILE)[:, None] < jnp.arange(ATT_TILE)[None, :]).astype(BF16)
    head_rows = (jnp.arange(8)[:, None] == jnp.arange(BW)[None, :] // HEAD_DIM).astype(BF16)
    row2 = lambda a: a.reshape(1, -1)

    h = x.reshape(t, D_MODEL)
    for i in range(DEPTH):
        w = w_in[i]
        w_main = jnp.concatenate([
            w[:, 0:256], w[:, 256:512] * scale, w[:, 512:768],
            w[:, 1028:2052],
            w[:, 2052:2308] * scale, w[:, 2308:2564]], axis=1).astype(BF16)
        w_vt = jnp.concatenate([w[:, 768:1024].T, w[:, 2564:2820].T], axis=0).astype(BF16)
        w_ff = _pad_rows(w[:, 1024:1028].T, 8, 0).astype(BF16)
        w_gates = w[:, 2820:].astype(BF16)

        u_sb, fqk, rw, sqk, fvt, svt, fft = _proj(h, w_main, w_vt, w_ff, bsz, seq, tm)

        bbar, cmat, lam = _s5_params(s5_lambda_re[i], s5_lambda_im[i], s5_log_dt[i], s5_b_re[i],
                                     s5_b_im[i], s5_c_re[i], s5_c_im[i], bsz)
        y_s5 = _s5(u_sb.reshape(seq * bsz, BW), bbar, cmat, lam, row2(s5_d[i]),
                   s5_glu_w[i].astype(BF16), row2(s5_glu_b[i]), bsz, seq, ts)
        y_s5 = y_s5.reshape(seq, bsz * BW)

        bias8 = _pad_rows(fox_f_bias[i].reshape(N_HEADS, 1), 8, 0)
        c_row = _fox_c(fft, bias8, tri_lane)
        ccol = jnp.transpose(c_row[:, :N_HEADS], (0, 2, 1)).reshape(t, N_HEADS)
        y_fox = _fox_attn(fqk, fvt, ccol, c_row, head_rows, bsz, seq)

        rparams = (row2(rwkv_mu[i]), row2(rwkv_w0[i]), _pad_rows(rwkv_w2[i], LANE, 0),
                   row2(rwkv_a0[i]), _pad_rows(rwkv_a2[i], LANE, 64), rwkv_g2[i],
                   row2(rwkv_k_k[i]), row2(rwkv_k_a[i]), row2(rwkv_r_k[i]),
                   row2(rwkv_lnx_g[i]), row2(rwkv_lnx_b[i]))
        y_rwkv = _rwkv(rw, rparams, rwkv_consts, bsz, seq, chunk, 8 if bsz % 8 == 0 else 1)

        y_sb = _sb_attn(sqk, svt, later_key, bsz, seq)

        h = _merge(h, y_s5, y_fox, y_rwkv, y_sb, w_gates, w_branch[i].astype(BF16),
                   w_out[i].astype(BF16), row2(ln1_g[i]), row2(ln1_b[i]), bsz, seq, tm)
        h = _mlp(h, p[i].reshape(t, PLE_DIM), mlp_w1[i].astype(BF16), mlp_w2[i].astype(BF16),
                 ple_w[i].astype(BF16), ple_gate_w[i].astype(BF16), row2(ln2_g[i]), row2(ln2_b[i]), tm)
    return h.reshape(bsz, seq, D_MODEL)
```
